```python
import jax, jax.numpy as jnp
from jax import lax
import numpy as np

D_MODEL = 2048
BATCH = 1
SEQ = 8192
DEPTH = 4
DEC_BATCH = 16
DEC_SEQ = 16
PAST_LEN = 2048

CHUNK = 64
N_A_LAYERS = DEPTH // 2
N_B_LAYERS = DEPTH - N_A_LAYERS
D_FF = 5632
GMLP_CHUNK = 128
D_GATE = 2 * D_MODEL
GMLP_GROUPS = 4
N_HEADS = 16
HEAD_DIM = D_MODEL // N_HEADS
Q_BLOCK = 128
RMS_EPS = 1e-6
LN_EPS = 1e-5
NEG_INF = -1e30

kernel_name = 'yoco_gmlp_fox_macaron_stream_step'


def rmsnorm(x, g):
    xf = x.astype(jnp.float32)
    y = xf * lax.rsqrt(jnp.mean(xf * xf, axis=-1, keepdims=True) + RMS_EPS)
    return (y * g.astype(jnp.float32)).astype(x.dtype)


def layernorm(x, g, b):
    xf = x.astype(jnp.float32)
    mu = jnp.mean(xf, axis=-1, keepdims=True)
    xc = xf - mu
    y = xc * lax.rsqrt(jnp.mean(xc * xc, axis=-1, keepdims=True) + LN_EPS)
    return (y * g.astype(jnp.float32) + b.astype(jnp.float32)).astype(x.dtype)


def half_ffn(x, g, w_gate, w_up, w_down):
    h = rmsnorm(x, g)
    return x + 0.5 * ((jax.nn.silu(h @ w_gate) * (h @ w_up)) @ w_down)


def gmlp_mix(h, w_in, ln_g, ln_b, w_s, b_s, w_out):
    bsz, s, _ = h.shape
    L = min(s, GMLP_CHUNK)
    z = jax.nn.gelu(h @ w_in, approximate=False)
    u, v = jnp.split(z, 2, axis=-1)
    vn = layernorm(v, ln_g, ln_b)
    tril = jnp.tril(jnp.ones((L, L), dtype=bool))
    ws = jnp.where(tril[None], w_s[:, :L, :L], 0).astype(vn.dtype)
    v5 = vn.reshape(bsz, s // L, L, GMLP_GROUPS, D_GATE // GMLP_GROUPS)
    mixed = jnp.einsum('gts,bnsgc->bntgc', ws, v5) + b_s[:, :L].T[:, :, None]
    gated = u * mixed.reshape(bsz, s, D_GATE)
    return gated @ w_out, vn


def shared_kv(x, kv_norm, w_k, w_v, w_f, b_f):
    bsz, s, _ = x.shape
    h = rmsnorm(x, kv_norm)
    k = (h @ w_k).reshape(bsz, s, N_HEADS, HEAD_DIM)
    v = (h @ w_v).reshape(bsz, s, N_HEADS, HEAD_DIM)
    logf = jax.nn.log_sigmoid((h @ w_f).astype(jnp.float32) + b_f.astype(jnp.float32)).astype(x.dtype)
    return k, v, logf


def fox_prompt(q, k, v, logf):
    bsz, s = q.shape[:2]
    nb = s // Q_BLOCK
    c = jnp.cumsum(logf.astype(jnp.float32), axis=1)
    c_k = jnp.transpose(c, (0, 2, 1))[:, :, None, :]
    k_pos = jnp.arange(s)
    q_blk = jnp.moveaxis(q.reshape(bsz, nb, Q_BLOCK, N_HEADS, HEAD_DIM), 1, 0)
    cq_blk = jnp.moveaxis(c.reshape(bsz, nb, Q_BLOCK, N_HEADS), 1, 0)
    pos_blk = jnp.arange(s).reshape(nb, Q_BLOCK)
    scale = HEAD_DIM ** -0.5

    def block(args):
        qb, cqb, qpos = args
        logits = jnp.einsum('bqhd,bkhd->bhqk', qb, k, preferred_element_type=jnp.float32) * scale
        logits = logits + jnp.transpose(cqb, (0, 2, 1))[..., None] - c_k
        logits = jnp.where(qpos[:, None] >= k_pos[None, :], logits, NEG_INF)
        p = jax.nn.softmax(logits, axis=-1).astype(v.dtype)
        return jnp.einsum('bhqk,bkhd->bqhd', p, v)

    out = lax.map(block, (q_blk, cq_blk, pos_blk))
    return jnp.moveaxis(out, 0, 1).reshape(bsz, s, N_HEADS * HEAD_DIM)


def fox_sample(q, k_all, v_all, logf_all, past_len):
    bsz, t = q.shape[:2]
    n = k_all.shape[1]
    c = jnp.cumsum(logf_all.astype(jnp.float32), axis=1)
    cq = c[:, past_len:]
    scale = HEAD_DIM ** -0.5
    logits = jnp.einsum('bqhd,bkhd->bhqk', q, k_all, preferred_element_type=jnp.float32) * scale
    logits = logits + jnp.transpose(cq, (0, 2, 1))[..., None] - jnp.transpose(c, (0, 2, 1))[:, :, None, :]
    mask = (past_len + jnp.arange(t))[:, None] >= jnp.arange(n)[None, :]
    logits = jnp.where(mask, logits, NEG_INF)
    p = jax.nn.softmax(logits, axis=-1).astype(v_all.dtype)
    out = jnp.einsum('bhqk,bkhd->bqhd', p, v_all)
    return out.reshape(bsz, t, N_HEADS * HEAD_DIM)


def run_trunk(x, p, cache):
    bsz, s, _ = x.shape
    gmlp_v = []
    for l in range(DEPTH):
        if l == N_A_LAYERS:
            k_new, v_new, logf_new = shared_kv(x, p['kv_norm'], p['w_k'], p['w_v'], p['w_f'], p['b_f'])
            if cache is None:
                ctx = (k_new, v_new, logf_new)
            else:
                ctx = (jnp.concatenate([cache[0], k_new], axis=1),
                       jnp.concatenate([cache[1], v_new], axis=1),
                       jnp.concatenate([cache[2], logf_new], axis=1))
        x = half_ffn(x, p['ffn1_norm'][l], p['ffn1_w_gate'][l], p['ffn1_w_up'][l], p['ffn1_w_down'][l])
        h = rmsnorm(x, p['mix_norm'][l])
        if l < N_A_LAYERS:
            a, vn = gmlp_mix(h, p['gmlp_w_in'][l], p['gmlp_ln_g'][l], p['gmlp_ln_b'][l],
                             p['gmlp_w_s'][l], p['gmlp_b_s'][l], p['gmlp_w_out'][l])
            gmlp_v.append(vn)
        else:
            j = l - N_A_LAYERS
            q = (h @ p['fox_w_q'][j]).reshape(bsz, s, N_HEADS, HEAD_DIM)
            if cache is None:
                o = fox_prompt(q, ctx[0], ctx[1], ctx[2])
            else:
                o = fox_sample(q, ctx[0], ctx[1], ctx[2], cache[0].shape[1])
            a = o @ p['fox_w_o'][j]
        x = x + a
        x = half_ffn(x, p['ffn2_norm'][l], p['ffn2_w_gate'][l], p['ffn2_w_up'][l], p['ffn2_w_down'][l])
    y = rmsnorm(x, p['final_norm'])
    return y, k_new, v_new, logf_new, gmlp_v


def setup_inputs(seed: int = 0) -> dict:
    key = jax.random.key(seed)
    ks = iter(jax.random.split(key, 40))

    def nrm(shape, scale):
        return scale * jax.random.normal(next(ks), shape, jnp.float32)

    def gain(shape):
        return 1.0 + 0.02 * jax.random.normal(next(ks), shape, jnp.float32)

    hw = N_HEADS * HEAD_DIM
    x_prompt = nrm((BATCH, SEQ, D_MODEL), 1.0)
    x_sample = nrm((DEC_BATCH, DEC_SEQ, D_MODEL), 1.0)
    cache_k = nrm((DEC_BATCH, PAST_LEN, N_HEADS, HEAD_DIM), 1.0)
    cache_v = nrm((DEC_BATCH, PAST_LEN, N_HEADS, HEAD_DIM), 1.0)
    b_f = jax.random.uniform(next(ks), (N_HEADS,), jnp.float32, 1.0, 6.0)
    cache_logf = jax.nn.log_sigmoid(b_f + nrm((DEC_BATCH, PAST_LEN, N_HEADS), 1.0))
    return {
        'x_prompt': x_prompt,
        'x_sample': x_sample,
        'cache_k': cache_k,
        'cache_v': cache_v,
        'cache_logf': cache_logf,
        'ffn1_norm': gain((DEPTH, D_MODEL)),
        'ffn1_w_gate': nrm((DEPTH, D_MODEL, D_FF), D_MODEL ** -0.5),
        'ffn1_w_up': nrm((DEPTH, D_MODEL, D_FF), D_MODEL ** -0.5),
        'ffn1_w_down': nrm((DEPTH, D_FF, D_MODEL), D_FF ** -0.5),
        'mix_norm': gain((DEPTH, D_MODEL)),
        'ffn2_norm': gain((DEPTH, D_MODEL)),
        'ffn2_w_gate': nrm((DEPTH, D_MODEL, D_FF), D_MODEL ** -0.5),
        'ffn2_w_up': nrm((DEPTH, D_MODEL, D_FF), D_MODEL ** -0.5),
        'ffn2_w_down': nrm((DEPTH, D_FF, D_MODEL), D_FF ** -0.5),
        'gmlp_w_in': nrm((N_A_LAYERS, D_MODEL, 2 * D_GATE), D_MODEL ** -0.5),
        'gmlp_ln_g': gain((N_A_LAYERS, D_GATE)),
        'gmlp_ln_b': nrm((N_A_LAYERS, D_GATE), 0.02),
        'gmlp_w_s': nrm((N_A_LAYERS, GMLP_GROUPS, GMLP_CHUNK, GMLP_CHUNK), GMLP_CHUNK ** -0.5),
        'gmlp_b_s': 1.0 + nrm((N_A_LAYERS, GMLP_GROUPS, GMLP_CHUNK), 0.1),
        'gmlp_w_out': nrm((N_A_LAYERS, D_GATE, D_MODEL), D_GATE ** -0.5),
        'kv_norm': gain((D_MODEL,)),
        'w_k': nrm((D_MODEL, hw), D_MODEL ** -0.5),
        'w_v': nrm((D_MODEL, hw), D_MODEL ** -0.5),
        'w_f': nrm((D_MODEL, N_HEADS), 0.5 * D_MODEL ** -0.5),
        'b_f': b_f,
        'fox_w_q': nrm((N_B_LAYERS, D_MODEL, hw), D_MODEL ** -0.5),
        'fox_w_o': nrm((N_B_LAYERS, hw, D_MODEL), hw ** -0.5),
        'final_norm': gain((D_MODEL,)),
    }


def reference(x_prompt, x_sample, cache_k, cache_v, cache_logf,
              ffn1_norm, ffn1_w_gate, ffn1_w_up, ffn1_w_down, mix_norm,
              ffn2_norm, ffn2_w_gate, ffn2_w_up, ffn2_w_down,
              gmlp_w_in, gmlp_ln_g, gmlp_ln_b, gmlp_w_s, gmlp_b_s, gmlp_w_out,
              kv_norm, w_k, w_v, w_f, b_f, fox_w_q, fox_w_o, final_norm):
    p = dict(ffn1_norm=ffn1_norm, ffn1_w_gate=ffn1_w_gate, ffn1_w_up=ffn1_w_up, ffn1_w_down=ffn1_w_down,
             mix_norm=mix_norm, ffn2_norm=ffn2_norm, ffn2_w_gate=ffn2_w_gate, ffn2_w_up=ffn2_w_up,
             ffn2_w_down=ffn2_w_down, gmlp_w_in=gmlp_w_in, gmlp_ln_g=gmlp_ln_g, gmlp_ln_b=gmlp_ln_b,
             gmlp_w_s=gmlp_w_s, gmlp_b_s=gmlp_b_s, gmlp_w_out=gmlp_w_out, kv_norm=kv_norm,
             w_k=w_k, w_v=w_v, w_f=w_f, b_f=b_f, fox_w_q=fox_w_q, fox_w_o=fox_w_o, final_norm=final_norm)
    y_prompt, k_prompt, v_prompt, logf_prompt, _ = run_trunk(x_prompt, p, None)
    y_sample, k_sample, v_sample, logf_sample, gv = run_trunk(x_sample, p, (cache_k, cache_v, cache_logf))
    gmlp_v_sample = jnp.stack(gv, axis=0)
    return (y_prompt, y_sample, k_prompt, v_prompt, logf_prompt,
            k_sample, v_sample, logf_sample, gmlp_v_sample)
```

```python
import functools
import math

import jax
import jax.numpy as jnp
from jax import lax
from jax.experimental import pallas as pl
from jax.experimental.pallas import tpu as pltpu

D_MODEL = 2048
SEQ = 8192
DEPTH = 4
DEC_BATCH = 16
DEC_SEQ = 16
PAST_LEN = 2048
N_A_LAYERS = DEPTH // 2
D_FF = 5632
GMLP_CHUNK = 128
D_GATE = 2 * D_MODEL
GMLP_GROUPS = 4
GROUP_W = D_GATE // GMLP_GROUPS
N_HEADS = 16
HEAD_DIM = D_MODEL // N_HEADS
RMS_EPS = 1e-6
LN_EPS = 1e-5
NEG_INF = -1e30

N_SAMPLE = DEC_BATCH * DEC_SEQ
M_ROWS = SEQ + N_SAMPLE
LANES = 128

F32 = jnp.float32
BF16 = jnp.bfloat16

ROW_TILE = 768
FF_TILE = 512
GMLP_ROW_TILE = 256
ATT_TILE = 512
CACHE_TILE = 512
VMEM_LIMIT = 56 * 1024 * 1024


def _params(*sem):
    return pltpu.CompilerParams(dimension_semantics=sem, vmem_limit_bytes=VMEM_LIMIT)


def _rms(xf, g):
    y = xf * lax.rsqrt(jnp.mean(xf * xf, axis=-1, keepdims=True) + RMS_EPS)
    return y * g


def _ffn_kernel(x_ref, g_ref, wg_ref, wu_ref, wd_ref, gf_ref, o_ref, h_ref, *, final_norm):
    j = pl.program_id(1)

    @pl.when(j == 0)
    def _():
        h_ref[...] = _rms(x_ref[...], g_ref[...]).astype(BF16)
        o_ref[...] = jnp.zeros_like(o_ref)

    h = h_ref[...]
    a = jnp.dot(h, wg_ref[...], preferred_element_type=F32)
    b = jnp.dot(h, wu_ref[...], preferred_element_type=F32)
    act = (a * jax.nn.sigmoid(a) * b).astype(BF16)
    o_ref[...] += jnp.dot(act, wd_ref[...], preferred_element_type=F32)

    @pl.when(j == pl.num_programs(1) - 1)
    def _():
        r = x_ref[...] + 0.5 * o_ref[...]
        if final_norm:
            r = _rms(r, gf_ref[...])
        o_ref[...] = r


def _half_ffn(x, g, wg, wu, wd, layer, gf, final_norm):
    m = x.shape[0]
    grid = (m // ROW_TILE, D_FF // FF_TILE)
    return pl.pallas_call(
        functools.partial(_ffn_kernel, final_norm=final_norm),
        grid=grid,
        in_specs=[
            pl.BlockSpec((ROW_TILE, D_MODEL), lambda i, j: (i, 0)),
            pl.BlockSpec((1, D_MODEL), lambda i, j: (0, 0)),
            pl.BlockSpec((None, D_MODEL, FF_TILE), lambda i, j: (layer, 0, j)),
            pl.BlockSpec((None, D_MODEL, FF_TILE), lambda i, j: (layer, 0, j)),
            pl.BlockSpec((None, FF_TILE, D_MODEL), lambda i, j: (layer, j, 0)),
            pl.BlockSpec((1, D_MODEL), lambda i, j: (0, 0)),
        ],
        out_specs=pl.BlockSpec((ROW_TILE, D_MODEL), lambda i, j: (i, 0)),
        out_shape=jax.ShapeDtypeStruct((m, D_MODEL), F32),
        scratch_shapes=[pltpu.VMEM((ROW_TILE, D_MODEL), BF16)],
        compiler_params=_params("parallel", "arbitrary"),
        name="half_ffn",
    )(x, g[layer].reshape(1, D_MODEL), wg, wu, wd, gf)


def _gelu(z):
    return 0.5 * z * (1.0 + lax.erf(z * (1.0 / math.sqrt(2.0))))


def _log_sigmoid(z):
    return -(jnp.maximum(-z, 0.0) + jnp.log1p(jnp.exp(-jnp.abs(z))))


def _norm_matmul_kernel(x_ref, g_ref, w_ref, b_ref, *rest, act, n_out):
    outs, h_ref = rest[:n_out], rest[n_out]
    j = pl.program_id(1)

    @pl.when(j == 0)
    def _():
        h_ref[...] = _rms(x_ref[...], g_ref[...]).astype(BF16)

    z = jnp.dot(h_ref[...], w_ref[...], preferred_element_type=F32)
    if act == "gelu":
        z = _gelu(z)
    elif act == "log_sigmoid":
        z = _log_sigmoid(z + b_ref[...])
    for o in outs:
        o[...] = z.astype(o.dtype)


def _norm_matmul(x, g, w, b, col_tile, act, out_dtypes):
    m = x.shape[0]
    n = w.shape[1]
    grid = (m // ROW_TILE, n // col_tile)
    out_spec = pl.BlockSpec((ROW_TILE, col_tile), lambda i, j: (i, j))
    return pl.pallas_call(
        functools.partial(_norm_matmul_kernel, act=act, n_out=len(out_dtypes)),
        grid=grid,
        in_specs=[
            pl.BlockSpec((ROW_TILE, D_MODEL), lambda i, j: (i, 0)),
            pl.BlockSpec((1, D_MODEL), lambda i, j: (0, 0)),
            pl.BlockSpec((D_MODEL, col_tile), lambda i, j: (0, j)),
            pl.BlockSpec((1, col_tile), lambda i, j: (0, j)),
        ],
        out_specs=[out_spec] * len(out_dtypes),
        out_shape=[jax.ShapeDtypeStruct((m, n), d) for d in out_dtypes],
        scratch_shapes=[pltpu.VMEM((ROW_TILE, D_MODEL), BF16)],
        compiler_params=_params("parallel", "arbitrary"),
        name="norm_matmul_" + str(act),
    )(x, g, w, b)


def _matmul_residual_kernel(a_ref, w_ref, x_ref, o_ref):
    o_ref[...] = x_ref[...] + jnp.dot(a_ref[...], w_ref[...], preferred_element_type=F32)


def _matmul_residual(a, w, x, col_tile):
    m, k = a.shape
    n = w.shape[1]
    return pl.pallas_call(
        _matmul_residual_kernel,
        grid=(m // ROW_TILE, n // col_tile),
        in_specs=[
            pl.BlockSpec((ROW_TILE, k), lambda i, j: (i, 0)),
            pl.BlockSpec((k, col_tile), lambda i, j: (0, j)),
            pl.BlockSpec((ROW_TILE, col_tile), lambda i, j: (i, j)),
        ],
        out_specs=pl.BlockSpec((ROW_TILE, col_tile), lambda i, j: (i, j)),
        out_shape=jax.ShapeDtypeStruct((m, n), F32),
        compiler_params=_params("parallel", "arbitrary"),
        name="matmul_residual",
    )(a, w, x)


def _gmlp_mix_kernel(u_ref, v_ref, x_ref, lng_ref, lnb_ref, ws_ref, bs_ref, wo_ref,
                     o_ref, vn_ref, gated_ref, *, n_prompt_tiles):
    i = pl.program_id(0)
    v = v_ref[...]
    mu = jnp.mean(v, axis=-1, keepdims=True)
    vc = v - mu
    vn = vc * lax.rsqrt(jnp.mean(vc * vc, axis=-1, keepdims=True) + LN_EPS)
    vn = vn * lng_ref[...] + lnb_ref[...]

    is_sample = i >= n_prompt_tiles

    @pl.when(is_sample)
    def _():
        vn_ref[...] = vn

    sel = is_sample.astype(jnp.int32)
    shift = jnp.where(is_sample, int(math.log2(DEC_SEQ)), int(math.log2(GMLP_CHUNK)))
    t = lax.broadcasted_iota(jnp.int32, (GMLP_CHUNK, GMLP_CHUNK), 0)
    s = lax.broadcasted_iota(jnp.int32, (GMLP_CHUNK, GMLP_CHUNK), 1)
    mask = (s <= t) & ((t >> shift) == (s >> shift))
    vnb = vn.astype(BF16)
    for grp in range(GMLP_GROUPS):
        w = jnp.where(mask, ws_ref[sel, grp], 0.0).astype(BF16)
        bias = bs_ref[sel, grp]
        cols = slice(grp * GROUP_W, (grp + 1) * GROUP_W)
        for c in range(GMLP_ROW_TILE // GMLP_CHUNK):
            rows = slice(c * GMLP_CHUNK, (c + 1) * GMLP_CHUNK)
            mixed = jnp.dot(w, vnb[rows, cols], preferred_element_type=F32) + bias
            gated_ref[rows, cols] = (u_ref[rows, cols] * mixed).astype(BF16)
    o_ref[...] = x_ref[...] + jnp.dot(gated_ref[...], wo_ref[...], preferred_element_type=F32)


def _gmlp_mix(z, x, ln_g, ln_b, ws2, bs2, w_out):
    m = x.shape[0]
    n_prompt_tiles = SEQ // GMLP_ROW_TILE
    single = pl.Buffered(1)
    return pl.pallas_call(
        functools.partial(_gmlp_mix_kernel, n_prompt_tiles=n_prompt_tiles),
        grid=(m // GMLP_ROW_TILE,),
        in_specs=[
            pl.BlockSpec((GMLP_ROW_TILE, D_GATE), lambda i: (i, 0)),
            pl.BlockSpec((GMLP_ROW_TILE, D_GATE), lambda i: (i, 1)),
            pl.BlockSpec((GMLP_ROW_TILE, D_MODEL), lambda i: (i, 0)),
            pl.BlockSpec((1, D_GATE), lambda i: (0, 0)),
            pl.BlockSpec((1, D_GATE), lambda i: (0, 0)),
            pl.BlockSpec((2, GMLP_GROUPS, GMLP_CHUNK, GMLP_CHUNK), lambda i: (0, 0, 0, 0)),
            pl.BlockSpec((2, GMLP_GROUPS, GMLP_CHUNK, 1), lambda i: (0, 0, 0, 0)),
            pl.BlockSpec((D_GATE, D_MODEL), lambda i: (0, 0), pipeline_mode=single),
        ],
        out_specs=[
            pl.BlockSpec((GMLP_ROW_TILE, D_MODEL), lambda i: (i, 0)),
            pl.BlockSpec((GMLP_ROW_TILE, D_GATE),
                         lambda i: (jnp.maximum(i - n_prompt_tiles, 0), 0), pipeline_mode=single),
        ],
        out_shape=[
            jax.ShapeDtypeStruct((m, D_MODEL), F32),
            jax.ShapeDtypeStruct((N_SAMPLE, D_GATE), F32),
        ],
        scratch_shapes=[pltpu.VMEM((GMLP_ROW_TILE, D_GATE), BF16)],
        compiler_params=_params("arbitrary"),
        name="gmlp_mix",
    )(z, z, x, ln_g, ln_b, ws2, bs2, w_out)


def _scan_kernel(x_ref, o_ref):
    x = x_ref[...]
    n = x.shape[-1]
    idx = lax.broadcasted_iota(jnp.int32, x.shape, 1)
    step = 1
    while step < n:
        x = x + jnp.where(idx >= step, pltpu.roll(x, step, axis=1), 0.0)
        step *= 2
    o_ref[...] = x


def _cumsum_lanes(x):
    return pl.pallas_call(
        _scan_kernel,
        out_shape=jax.ShapeDtypeStruct(x.shape, F32),
        compiler_params=pltpu.CompilerParams(vmem_limit_bytes=VMEM_LIMIT),
        name="cumsum_lanes",
    )(x)


def _softmax_step(logits, v, m_ref, l_ref, acc_ref):
    m_prev = m_ref[...]
    m_new = jnp.maximum(m_prev, jnp.max(logits, axis=-1, keepdims=True))
    alpha = jnp.exp(m_prev - m_new)
    p = jnp.exp(logits - m_new)
    l_ref[...] = alpha * l_ref[...] + jnp.sum(p, axis=-1, keepdims=True)
    acc_ref[...] = alpha * acc_ref[...] + jnp.dot(p.astype(BF16), v, preferred_element_type=F32)
    m_ref[...] = m_new


def _fox_prompt_kernel(q_ref, k_ref, v_ref, c_ref, o_ref, m_ref, l_ref, acc_ref, cq_ref):
    qi = pl.program_id(1)
    scale = HEAD_DIM ** -0.5
    q = q_ref[...]
    cq_ref[...] = jnp.broadcast_to(c_ref[qi], (LANES, ATT_TILE)).T
    m_ref[...] = jnp.full_like(m_ref, NEG_INF)
    l_ref[...] = jnp.zeros_like(l_ref)
    acc_ref[...] = jnp.zeros_like(acc_ref)

    def logits_of(kj):
        start = pl.multiple_of(kj * ATT_TILE, ATT_TILE)
        k = k_ref[pl.ds(start, ATT_TILE), :]
        s = lax.dot_general(q, k, (((1,), (1,)), ((), ())), preferred_element_type=F32)
        decay = cq_ref[:, 0:1] - c_ref[kj]
        return s * scale + decay, v_ref[pl.ds(start, ATT_TILE), :]

    def body(kj, carry):
        logits, v = logits_of(kj)
        _softmax_step(logits, v, m_ref, l_ref, acc_ref)
        return carry

    lax.fori_loop(0, qi, body, 0)

    logits, v = logits_of(qi)
    row = lax.broadcasted_iota(jnp.int32, (ATT_TILE, ATT_TILE), 0)
    col = lax.broadcasted_iota(jnp.int32, (ATT_TILE, ATT_TILE), 1)
    logits = jnp.where(row >= col, logits, NEG_INF)
    _softmax_step(logits, v, m_ref, l_ref, acc_ref)
    o_ref[...] = (acc_ref[...] / l_ref[...]).astype(o_ref.dtype)


def _fox_prompt(q, k, v, c_tiles):
    n_tiles = SEQ // ATT_TILE
    return pl.pallas_call(
        _fox_prompt_kernel,
        grid=(N_HEADS, n_tiles),
        in_specs=[
            pl.BlockSpec((ATT_TILE, HEAD_DIM), lambda h, i: (i, h)),
            pl.BlockSpec((SEQ, HEAD_DIM), lambda h, i: (0, h)),
            pl.BlockSpec((SEQ, HEAD_DIM), lambda h, i: (0, h)),
            pl.BlockSpec((None, n_tiles, 1, ATT_TILE), lambda h, i: (h, 0, 0, 0)),
        ],
        out_specs=pl.BlockSpec((ATT_TILE, HEAD_DIM), lambda h, i: (i, h)),
        out_shape=jax.ShapeDtypeStruct((SEQ, D_MODEL), BF16),
        scratch_shapes=[
            pltpu.VMEM((ATT_TILE, 1), F32),
            pltpu.VMEM((ATT_TILE, 1), F32),
            pltpu.VMEM((ATT_TILE, HEAD_DIM), F32),
            pltpu.VMEM((ATT_TILE, LANES), F32),
        ],
        compiler_params=_params("parallel", "arbitrary"),
        name="fox_prompt",
    )(q, k, v, c_tiles)


def _fox_sample_kernel(q_ref, kc_ref, vc_ref, kn_ref, vn_ref, cc_ref, cn_ref, o_ref,
                       m_ref, l_ref, acc_ref):
    j = pl.program_id(1)
    scale = HEAD_DIM ** -0.5

    @pl.when(j == 0)
    def _():
        m_ref[...] = jnp.full_like(m_ref, NEG_INF)
        l_ref[...] = jnp.zeros_like(l_ref)
        acc_ref[...] = jnp.zeros_like(acc_ref)

    c_new = cn_ref[...]
    c_pad = jnp.concatenate([c_new, jnp.zeros((LANES - N_HEADS, LANES), F32)], axis=0)
    cq_all = c_pad.T[:DEC_SEQ, :]

    for h in range(N_HEADS):
        cols = slice(h * HEAD_DIM, (h + 1) * HEAD_DIM)
        q = q_ref[:, cols]
        cq = cq_all[:, h:h + 1]
        k = kc_ref[:, cols].astype(BF16)
        s = lax.dot_general(q, k, (((1,), (1,)), ((), ())), preferred_element_type=F32)
        logits = s * scale + (cq - cc_ref[h:h + 1, :])
        _softmax_step(logits, vc_ref[:, cols].astype(BF16), m_ref.at[h], l_ref.at[h], acc_ref.at[h])

    @pl.when(j == pl.num_programs(1) - 1)
    def _():
        row = lax.broadcasted_iota(jnp.int32, (DEC_SEQ, LANES), 0)
        col = lax.broadcasted_iota(jnp.int32, (DEC_SEQ, LANES), 1)
        pad = jnp.zeros((LANES - DEC_SEQ, HEAD_DIM), BF16)
        for h in range(N_HEADS):
            cols = slice(h * HEAD_DIM, (h + 1) * HEAD_DIM)
            q = q_ref[:, cols]
            cq = cq_all[:, h:h + 1]
            k = jnp.concatenate([kn_ref[:, cols], pad], axis=0)
            v = jnp.concatenate([vn_ref[:, cols], pad], axis=0)
            s = lax.dot_general(q, k, (((1,), (1,)), ((), ())), preferred_element_type=F32)
            logits = s * scale + (cq - c_new[h:h + 1, :])
            logits = jnp.where(row >= col, logits, NEG_INF)
            _softmax_step(logits, v, m_ref.at[h], l_ref.at[h], acc_ref.at[h])
            o_ref[:, cols] = (acc_ref[h] / l_ref[h]).astype(o_ref.dtype)


def _fox_sample(q, k_new, v_new, cache_k, cache_v, c_sample):
    first = SEQ // DEC_SEQ
    n_steps = PAST_LEN // CACHE_TILE
    new_spec = pl.BlockSpec((DEC_SEQ, D_MODEL), lambda b, j: (first + b, 0))
    cache_spec = pl.BlockSpec((None, CACHE_TILE, D_MODEL), lambda b, j: (b, j, 0))
    return pl.pallas_call(
        _fox_sample_kernel,
        grid=(DEC_BATCH, n_steps),
        in_specs=[
            new_spec, cache_spec, cache_spec, new_spec, new_spec,
            pl.BlockSpec((None, N_HEADS, CACHE_TILE), lambda b, j: (b, 0, j)),
            pl.BlockSpec((None, N_HEADS, LANES), lambda b, j: (b, 0, PAST_LEN // LANES)),
        ],
        out_specs=pl.BlockSpec((DEC_SEQ, D_MODEL), lambda b, j: (b, 0)),
        out_shape=jax.ShapeDtypeStruct((N_SAMPLE, D_MODEL), BF16),
        scratch_shapes=[
            pltpu.VMEM((N_HEADS, DEC_SEQ, 1), F32),
            pltpu.VMEM((N_HEADS, DEC_SEQ, 1), F32),
            pltpu.VMEM((N_HEADS, DEC_SEQ, HEAD_DIM), F32),
        ],
        compiler_params=_params("parallel", "arbitrary"),
        name="fox_sample",
    )(q, cache_k, cache_v, k_new, v_new, c_sample, c_sample)


def kernel(x_prompt, x_sample, cache_k, cache_v, cache_logf, ffn1_norm, ffn1_w_gate, ffn1_w_up,
           ffn1_w_down, mix_norm, ffn2_norm, ffn2_w_gate, ffn2_w_up, ffn2_w_down, gmlp_w_in,
           gmlp_ln_g, gmlp_ln_b, gmlp_w_s, gmlp_b_s, gmlp_w_out, kv_norm, w_k, w_v, w_f, b_f,
           fox_w_q, fox_w_o, final_norm):
    bf = lambda w: w.astype(BF16)
    x = jnp.concatenate([x_prompt.reshape(SEQ, D_MODEL), x_sample.reshape(N_SAMPLE, D_MODEL)], axis=0)

    ffn1 = (ffn1_norm, bf(ffn1_w_gate), bf(ffn1_w_up), bf(ffn1_w_down))
    ffn2 = (ffn2_norm, bf(ffn2_w_gate), bf(ffn2_w_up), bf(ffn2_w_down))
    final_g = final_norm.reshape(1, D_MODEL)
    no_bias_model = jnp.zeros((1, D_MODEL), F32)

    reps = GMLP_CHUNK // DEC_SEQ
    ws2 = jnp.stack([gmlp_w_s, jnp.tile(gmlp_w_s[:, :, :DEC_SEQ, :DEC_SEQ], (1, 1, reps, reps))], axis=1)
    bs2 = jnp.stack([gmlp_b_s, jnp.tile(gmlp_b_s[:, :, :DEC_SEQ], (1, 1, reps))], axis=1)[..., None]

    gmlp_v = []
    k_all = v_all = logf_all = None
    for l in range(DEPTH):
        if l == N_A_LAYERS:
            kvg = kv_norm.reshape(1, D_MODEL)
            k_all, k_bf = _norm_matmul(x, kvg, bf(w_k), no_bias_model, 1024, None, (F32, BF16))
            v_all, v_bf = _norm_matmul(x, kvg, bf(w_v), no_bias_model, 1024, None, (F32, BF16))
            w_f_pad = jnp.pad(bf(w_f), ((0, 0), (0, LANES - N_HEADS)))
            b_f_pad = jnp.pad(b_f, (0, LANES - N_HEADS)).reshape(1, LANES)
            (logf_pad,) = _norm_matmul(x, kvg, w_f_pad, b_f_pad, LANES, "log_sigmoid", (F32,))
            logf_all = logf_pad[:, :N_HEADS]
            c_prompt = _cumsum_lanes(logf_all[:SEQ].T)
            c_prompt = c_prompt.reshape(N_HEADS, SEQ // ATT_TILE, 1, ATT_TILE)
            lf_new = logf_all[SEQ:].reshape(DEC_BATCH, DEC_SEQ, N_HEADS).transpose(0, 2, 1)
            lf_cache = cache_logf.transpose(0, 2, 1)
            lf = jnp.concatenate(
                [lf_cache, lf_new, jnp.zeros((DEC_BATCH, N_HEADS, LANES - DEC_SEQ), F32)], axis=-1)
            c_sample = _cumsum_lanes(lf.reshape(DEC_BATCH * N_HEADS, PAST_LEN + LANES))
            c_sample = c_sample.reshape(DEC_BATCH, N_HEADS, PAST_LEN + LANES)
            cache_k2 = cache_k.reshape(DEC_BATCH, PAST_LEN, D_MODEL)
            cache_v2 = cache_v.reshape(DEC_BATCH, PAST_LEN, D_MODEL)

        x = _half_ffn(x, *ffn1, l, final_g, False)
        mg = mix_norm[l].reshape(1, D_MODEL)
        if l < N_A_LAYERS:
            (z,) = _norm_matmul(x, mg, bf(gmlp_w_in[l]), jnp.zeros((1, 2 * D_GATE), F32), 1024, "gelu", (F32,))
            x, vn = _gmlp_mix(z, x, gmlp_ln_g[l].reshape(1, D_GATE), gmlp_ln_b[l].reshape(1, D_GATE),
                              ws2[l], bs2[l], bf(gmlp_w_out[l]))
            gmlp_v.append(vn.reshape(DEC_BATCH, DEC_SEQ, D_GATE))
        else:
            jj = l - N_A_LAYERS
            (q,) = _norm_matmul(x, mg, bf(fox_w_q[jj]), no_bias_model, 1024, None, (BF16,))
            o_prompt = _fox_prompt(q, k_bf, v_bf, c_prompt)
            o_sample = _fox_sample(q, k_bf, v_bf, cache_k2, cache_v2, c_sample)
            o = jnp.concatenate([o_prompt, o_sample], axis=0)
            x = _matmul_residual(o, bf(fox_w_o[jj]), x, 1024)
        x = _half_ffn(x, *ffn2, l, final_g, l == DEPTH - 1)

    hd = (N_HEADS, HEAD_DIM)
    return (x[:SEQ].reshape(1, SEQ, D_MODEL),
            x[SEQ:].reshape(DEC_BATCH, DEC_SEQ, D_MODEL),
            k_all[:SEQ].reshape(1, SEQ, *hd),
            v_all[:SEQ].reshape(1, SEQ, *hd),
            logf_all[:SEQ].reshape(1, SEQ, N_HEADS),
            k_all[SEQ:].reshape(DEC_BATCH, DEC_SEQ, *hd),
            v_all[SEQ:].reshape(DEC_BATCH, DEC_SEQ, *hd),
            logf_all[SEQ:].reshape(DEC_BATCH, DEC_SEQ, N_HEADS),
            jnp.stack(gmlp_v, axis=0))
```

```python
import functools
import math

import jax
import jax.numpy as jnp
from jax import lax
from jax.experimental import pallas as pl
from jax.experimental.pallas import tpu as pltpu

D_MODEL = 2048
SEQ = 8192
DEPTH = 4
DEC_BATCH = 16
DEC_SEQ = 16
PAST_LEN = 2048
N_A_LAYERS = DEPTH // 2
D_FF = 5632
GMLP_CHUNK = 128
D_GATE = 2 * D_MODEL
GMLP_GROUPS = 4
GROUP_W = D_GATE // GMLP_GROUPS
N_HEADS = 16
HEAD_DIM = D_MODEL // N_HEADS
RMS_EPS = 1e-6
LN_EPS = 1e-5
NEG_INF = -1e30
LOG2E = math.log2(math.e)

N_SAMPLE = DEC_BATCH * DEC_SEQ
M_ROWS = SEQ + N_SAMPLE
LANES = 128

F32 = jnp.float32
BF16 = jnp.bfloat16

ROW_TILE = 768
FF_TILE = 512
GMLP_ROW_TILE = 256
ATT_TILE = 512
HEADS_PER_STEP = 4
CACHE_TILE = 512
VMEM_LIMIT = 56 * 1024 * 1024


def _params(*sem):
    return pltpu.CompilerParams(dimension_semantics=sem, vmem_limit_bytes=VMEM_LIMIT)


def _rms(xf, g):
    y = xf * lax.rsqrt(jnp.mean(xf * xf, axis=-1, keepdims=True) + RMS_EPS)
    return y * g


def _ffn_kernel(x_ref, g_ref, wg_ref, wu_ref, wd_ref, gf_ref, o_ref, h_ref, *, final_norm):
    j = pl.program_id(1)

    @pl.when(j == 0)
    def _():
        h_ref[...] = _rms(x_ref[...], g_ref[...]).astype(BF16)
        o_ref[...] = jnp.zeros_like(o_ref)

    h = h_ref[...]
    a = jnp.dot(h, wg_ref[...], preferred_element_type=F32)
    b = jnp.dot(h, wu_ref[...], preferred_element_type=F32)
    act = (a * jax.nn.sigmoid(a) * b).astype(BF16)
    o_ref[...] += jnp.dot(act, wd_ref[...], preferred_element_type=F32)

    @pl.when(j == pl.num_programs(1) - 1)
    def _():
        r = x_ref[...] + 0.5 * o_ref[...]
        if final_norm:
            r = _rms(r, gf_ref[...])
        o_ref[...] = r


def _half_ffn(x, g, wg, wu, wd, layer, gf, final_norm):
    m = x.shape[0]
    grid = (m // ROW_TILE, D_FF // FF_TILE)
    return pl.pallas_call(
        functools.partial(_ffn_kernel, final_norm=final_norm),
        grid=grid,
        in_specs=[
            pl.BlockSpec((ROW_TILE, D_MODEL), lambda i, j: (i, 0)),
            pl.BlockSpec((1, D_MODEL), lambda i, j: (0, 0)),
            pl.BlockSpec((None, D_MODEL, FF_TILE), lambda i, j: (layer, 0, j)),
            pl.BlockSpec((None, D_MODEL, FF_TILE), lambda i, j: (layer, 0, j)),
            pl.BlockSpec((None, FF_TILE, D_MODEL), lambda i, j: (layer, j, 0)),
            pl.BlockSpec((1, D_MODEL), lambda i, j: (0, 0)),
        ],
        out_specs=pl.BlockSpec((ROW_TILE, D_MODEL), lambda i, j: (i, 0)),
        out_shape=jax.ShapeDtypeStruct((m, D_MODEL), F32),
        scratch_shapes=[pltpu.VMEM((ROW_TILE, D_MODEL), BF16)],
        compiler_params=_params("parallel", "arbitrary"),
        name="half_ffn",
    )(x, g[layer].reshape(1, D_MODEL), wg, wu, wd, gf)


def _gelu(z):
    return 0.5 * z * (1.0 + lax.erf(z * (1.0 / math.sqrt(2.0))))


def _log_sigmoid(z):
    return -(jnp.maximum(-z, 0.0) + jnp.log1p(jnp.exp(-jnp.abs(z))))


def _norm_matmul_kernel(x_ref, g_ref, w_ref, b_ref, *rest, act, n_out, out_scale):
    outs, h_ref = rest[:n_out], rest[n_out]
    j = pl.program_id(1)

    @pl.when(j == 0)
    def _():
        h_ref[...] = _rms(x_ref[...], g_ref[...]).astype(BF16)

    z = jnp.dot(h_ref[...], w_ref[...], preferred_element_type=F32)
    if act == "gelu":
        z = _gelu(z)
    elif act == "log_sigmoid":
        z = _log_sigmoid(z + b_ref[...])
    if out_scale != 1.0:
        z = z * out_scale
    for o in outs:
        o[...] = z.astype(o.dtype)


def _norm_matmul(x, g, w, b, col_tile, act, out_dtypes, out_scale=1.0):
    m = x.shape[0]
    n = w.shape[1]
    grid = (m // ROW_TILE, n // col_tile)
    out_spec = pl.BlockSpec((ROW_TILE, col_tile), lambda i, j: (i, j))
    return pl.pallas_call(
        functools.partial(_norm_matmul_kernel, act=act, n_out=len(out_dtypes), out_scale=out_scale),
        grid=grid,
        in_specs=[
            pl.BlockSpec((ROW_TILE, D_MODEL), lambda i, j: (i, 0)),
            pl.BlockSpec((1, D_MODEL), lambda i, j: (0, 0)),
            pl.BlockSpec((D_MODEL, col_tile), lambda i, j: (0, j)),
            pl.BlockSpec((1, col_tile), lambda i, j: (0, j)),
        ],
        out_specs=[out_spec] * len(out_dtypes),
        out_shape=[jax.ShapeDtypeStruct((m, n), d) for d in out_dtypes],
        scratch_shapes=[pltpu.VMEM((ROW_TILE, D_MODEL), BF16)],
        compiler_params=_params("parallel", "arbitrary"),
        name="norm_matmul_" + str(act),
    )(x, g, w, b)


def _matmul_residual_kernel(a_ref, w_ref, x_ref, o_ref):
    o_ref[...] = x_ref[...] + jnp.dot(a_ref[...], w_ref[...], preferred_element_type=F32)


def _matmul_residual(a, w, x, col_tile):
    m, k = a.shape
    n = w.shape[1]
    return pl.pallas_call(
        _matmul_residual_kernel,
        grid=(m // ROW_TILE, n // col_tile),
        in_specs=[
            pl.BlockSpec((ROW_TILE, k), lambda i, j: (i, 0)),
            pl.BlockSpec((k, col_tile), lambda i, j: (0, j)),
            pl.BlockSpec((ROW_TILE, col_tile), lambda i, j: (i, j)),
        ],
        out_specs=pl.BlockSpec((ROW_TILE, col_tile), lambda i, j: (i, j)),
        out_shape=jax.ShapeDtypeStruct((m, n), F32),
        compiler_params=_params("parallel", "arbitrary"),
        name="matmul_residual",
    )(a, w, x)


def _gmlp_mix_kernel(u_ref, v_ref, x_ref, lng_ref, lnb_ref, ws_ref, bs_ref, wo_ref,
                     o_ref, vn_ref, gated_ref, *, n_prompt_tiles):
    i = pl.program_id(0)
    v = v_ref[...]
    mu = jnp.mean(v, axis=-1, keepdims=True)
    vc = v - mu
    vn = vc * lax.rsqrt(jnp.mean(vc * vc, axis=-1, keepdims=True) + LN_EPS)
    vn = vn * lng_ref[...] + lnb_ref[...]

    is_sample = i >= n_prompt_tiles

    @pl.when(is_sample)
    def _():
        vn_ref[...] = vn

    sel = is_sample.astype(jnp.int32)
    shift = jnp.where(is_sample, int(math.log2(DEC_SEQ)), int(math.log2(GMLP_CHUNK)))
    t = lax.broadcasted_iota(jnp.int32, (GMLP_CHUNK, GMLP_CHUNK), 0)
    s = lax.broadcasted_iota(jnp.int32, (GMLP_CHUNK, GMLP_CHUNK), 1)
    mask = (s <= t) & ((t >> shift) == (s >> shift))
    vnb = vn.astype(BF16)
    for grp in range(GMLP_GROUPS):
        w = jnp.where(mask, ws_ref[sel, grp], 0.0).astype(BF16)
        bias = bs_ref[sel, grp]
        cols = slice(grp * GROUP_W, (grp + 1) * GROUP_W)
        for c in range(GMLP_ROW_TILE // GMLP_CHUNK):
            rows = slice(c * GMLP_CHUNK, (c + 1) * GMLP_CHUNK)
            mixed = jnp.dot(w, vnb[rows, cols], preferred_element_type=F32) + bias
            gated_ref[rows, cols] = (u_ref[rows, cols] * mixed).astype(BF16)
    o_ref[...] = x_ref[...] + jnp.dot(gated_ref[...], wo_ref[...], preferred_element_type=F32)


def _gmlp_mix(z, x, ln_g, ln_b, ws2, bs2, w_out):
    m = x.shape[0]
    n_prompt_tiles = SEQ // GMLP_ROW_TILE
    single = pl.Buffered(1)
    return pl.pallas_call(
        functools.partial(_gmlp_mix_kernel, n_prompt_tiles=n_prompt_tiles),
        grid=(m // GMLP_ROW_TILE,),
        in_specs=[
            pl.BlockSpec((GMLP_ROW_TILE, D_GATE), lambda i: (i, 0)),
            pl.BlockSpec((GMLP_ROW_TILE, D_GATE), lambda i: (i, 1)),
            pl.BlockSpec((GMLP_ROW_TILE, D_MODEL), lambda i: (i, 0)),
            pl.BlockSpec((1, D_GATE), lambda i: (0, 0)),
            pl.BlockSpec((1, D_GATE), lambda i: (0, 0)),
            pl.BlockSpec((2, GMLP_GROUPS, GMLP_CHUNK, GMLP_CHUNK), lambda i: (0, 0, 0, 0)),
            pl.BlockSpec((2, GMLP_GROUPS, GMLP_CHUNK, 1), lambda i: (0, 0, 0, 0)),
            pl.BlockSpec((D_GATE, D_MODEL), lambda i: (0, 0), pipeline_mode=single),
        ],
        out_specs=[
            pl.BlockSpec((GMLP_ROW_TILE, D_MODEL), lambda i: (i, 0)),
            pl.BlockSpec((GMLP_ROW_TILE, D_GATE),
                         lambda i: (jnp.maximum(i - n_prompt_tiles, 0), 0), pipeline_mode=single),
        ],
        out_shape=[
            jax.ShapeDtypeStruct((m, D_MODEL), F32),
            jax.ShapeDtypeStruct((N_SAMPLE, D_GATE), F32),
        ],
        scratch_shapes=[pltpu.VMEM((GMLP_ROW_TILE, D_GATE), BF16)],
        compiler_params=_params("arbitrary"),
        name="gmlp_mix",
    )(z, z, x, ln_g, ln_b, ws2, bs2, w_out)


def _scan_kernel(x_ref, o_ref):
    x = x_ref[...]
    n = x.shape[-1]
    idx = lax.broadcasted_iota(jnp.int32, x.shape, 1)
    step = 1
    while step < n:
        x = x + jnp.where(idx >= step, pltpu.roll(x, step, axis=1), 0.0)
        step *= 2
    o_ref[...] = x * LOG2E


def _cumsum_lanes(x):
    return pl.pallas_call(
        _scan_kernel,
        out_shape=jax.ShapeDtypeStruct(x.shape, F32),
        compiler_params=pltpu.CompilerParams(vmem_limit_bytes=VMEM_LIMIT),
        name="cumsum_lanes",
    )(x)


def _tile_lanes(x, reps):
    return jnp.concatenate([x] * reps, axis=1)


def _softmax_step(logits, v, m_ref, acc_ref):
    reps = logits.shape[1] // LANES
    m_prev = m_ref[...]
    m_new = jnp.maximum(m_prev, jnp.max(logits, axis=1, keepdims=True))
    p = jnp.exp2(logits - _tile_lanes(m_new, reps))
    alpha = jnp.exp2(m_prev - m_new)
    v_ones = jnp.concatenate([v, jnp.ones_like(v)], axis=1)
    pv = jnp.dot(p.astype(BF16), v_ones, preferred_element_type=F32)
    acc_ref[...] = _tile_lanes(alpha, 2) * acc_ref[...] + pv
    m_ref[...] = m_new


def _softmax_result(acc_ref):
    acc = acc_ref[...]
    return acc[:, :HEAD_DIM] / acc[:, HEAD_DIM:]


def _fox_prompt_kernel(q_ref, k_ref, v_ref, c_ref, o_ref, m_ref, acc_ref, cq_ref):
    qi = pl.program_id(1)
    reps = ATT_TILE // LANES
    for hh in range(HEADS_PER_STEP):
        cq_ref[hh] = jnp.broadcast_to(c_ref[hh, qi], (LANES, ATT_TILE)).T
    m_ref[...] = jnp.full_like(m_ref, NEG_INF)
    acc_ref[...] = jnp.zeros_like(acc_ref)

    def block(kj, on_diagonal):
        start = pl.multiple_of(kj * ATT_TILE, ATT_TILE)
        for hh in range(HEADS_PER_STEP):
            cols = slice(hh * HEAD_DIM, (hh + 1) * HEAD_DIM)
            k = k_ref[pl.ds(start, ATT_TILE), cols]
            s = lax.dot_general(q_ref[:, cols], k, (((1,), (1,)), ((), ())), preferred_element_type=F32)
            logits = s + (_tile_lanes(cq_ref[hh], reps) - c_ref[hh, kj])
            if on_diagonal:
                row = lax.broadcasted_iota(jnp.int32, (ATT_TILE, ATT_TILE), 0)
                col = lax.broadcasted_iota(jnp.int32, (ATT_TILE, ATT_TILE), 1)
                logits = jnp.where(row >= col, logits, NEG_INF)
            _softmax_step(logits, v_ref[pl.ds(start, ATT_TILE), cols], m_ref.at[hh], acc_ref.at[hh])

    def body(kj, carry):
        block(kj, False)
        return carry

    lax.fori_loop(0, qi, body, 0)
    block(qi, True)
    for hh in range(HEADS_PER_STEP):
        cols = slice(hh * HEAD_DIM, (hh + 1) * HEAD_DIM)
        o_ref[:, cols] = _softmax_result(acc_ref.at[hh]).astype(o_ref.dtype)


def _fox_prompt(q, k, v, c_tiles):
    n_tiles = SEQ // ATT_TILE
    width = HEADS_PER_STEP * HEAD_DIM
    return pl.pallas_call(
        _fox_prompt_kernel,
        grid=(N_HEADS // HEADS_PER_STEP, n_tiles),
        in_specs=[
            pl.BlockSpec((ATT_TILE, width), lambda h, i: (i, h)),
            pl.BlockSpec((SEQ, width), lambda h, i: (0, h)),
            pl.BlockSpec((SEQ, width), lambda h, i: (0, h)),
            pl.BlockSpec((HEADS_PER_STEP, n_tiles, 1, ATT_TILE), lambda h, i: (h, 0, 0, 0)),
        ],
        out_specs=pl.BlockSpec((ATT_TILE, width), lambda h, i: (i, h)),
        out_shape=jax.ShapeDtypeStruct((SEQ, D_MODEL), BF16),
        scratch_shapes=[
            pltpu.VMEM((HEADS_PER_STEP, ATT_TILE, LANES), F32),
            pltpu.VMEM((HEADS_PER_STEP, ATT_TILE, 2 * HEAD_DIM), F32),
            pltpu.VMEM((HEADS_PER_STEP, ATT_TILE, LANES), F32),
        ],
        compiler_params=_params("parallel", "arbitrary"),
        name="fox_prompt",
    )(q, k, v, c_tiles)


def _fox_sample_kernel(q_ref, kc_ref, vc_ref, kn_ref, vn_ref, cc_ref, cn_ref, o_ref, m_ref, acc_ref):
    j = pl.program_id(1)

    @pl.when(j == 0)
    def _():
        m_ref[...] = jnp.full_like(m_ref, NEG_INF)
        acc_ref[...] = jnp.zeros_like(acc_ref)

    c_new = cn_ref[...]
    c_pad = jnp.concatenate([c_new, jnp.zeros((LANES - N_HEADS, LANES), F32)], axis=0)
    cq_all = c_pad.T[:DEC_SEQ, :]

    def cq_of(h):
        return jnp.broadcast_to(cq_all[:, h:h + 1], (DEC_SEQ, LANES))

    for h in range(N_HEADS):
        cols = slice(h * HEAD_DIM, (h + 1) * HEAD_DIM)
        head_rows = pl.ds(h, CACHE_TILE, stride=N_HEADS)
        k = kc_ref[head_rows, :].astype(BF16)
        s = lax.dot_general(q_ref[:, cols], k, (((1,), (1,)), ((), ())), preferred_element_type=F32)
        logits = s + (_tile_lanes(cq_of(h), CACHE_TILE // LANES) - cc_ref[h:h + 1, :])
        _softmax_step(logits, vc_ref[head_rows, :].astype(BF16), m_ref.at[h], acc_ref.at[h])

    @pl.when(j == pl.num_programs(1) - 1)
    def _():
        row = lax.broadcasted_iota(jnp.int32, (DEC_SEQ, LANES), 0)
        col = lax.broadcasted_iota(jnp.int32, (DEC_SEQ, LANES), 1)
        pad = jnp.zeros((LANES - DEC_SEQ, HEAD_DIM), BF16)
        for h in range(N_HEADS):
            cols = slice(h * HEAD_DIM, (h + 1) * HEAD_DIM)
            k = jnp.concatenate([kn_ref[:, cols], pad], axis=0)
            v = jnp.concatenate([vn_ref[:, cols], pad], axis=0)
            s = lax.dot_general(q_ref[:, cols], k, (((1,), (1,)), ((), ())), preferred_element_type=F32)
            logits = s + (cq_of(h) - c_new[h:h + 1, :])
            logits = jnp.where(row >= col, logits, NEG_INF)
            _softmax_step(logits, v, m_ref.at[h], acc_ref.at[h])
            o_ref[:, cols] = _softmax_result(acc_ref.at[h]).astype(o_ref.dtype)


def _fox_sample(q, k_new, v_new, cache_k, cache_v, c_sample):
    first = SEQ // DEC_SEQ
    n_steps = PAST_LEN // CACHE_TILE
    new_spec = pl.BlockSpec((DEC_SEQ, D_MODEL), lambda b, j: (first + b, 0))
    cache_spec = pl.BlockSpec((None, CACHE_TILE * N_HEADS, HEAD_DIM), lambda b, j: (b, j, 0))
    cache_k = cache_k.reshape(DEC_BATCH, PAST_LEN * N_HEADS, HEAD_DIM)
    cache_v = cache_v.reshape(DEC_BATCH, PAST_LEN * N_HEADS, HEAD_DIM)
    return pl.pallas_call(
        _fox_sample_kernel,
        grid=(DEC_BATCH, n_steps),
        in_specs=[
            new_spec, cache_spec, cache_spec, new_spec, new_spec,
            pl.BlockSpec((None, N_HEADS, CACHE_TILE), lambda b, j: (b, 0, j)),
            pl.BlockSpec((None, N_HEADS, LANES), lambda b, j: (b, 0, PAST_LEN // LANES)),
        ],
        out_specs=pl.BlockSpec((DEC_SEQ, D_MODEL), lambda b, j: (b, 0)),
        out_shape=jax.ShapeDtypeStruct((N_SAMPLE, D_MODEL), BF16),
        scratch_shapes=[
            pltpu.VMEM((N_HEADS, DEC_SEQ, LANES), F32),
            pltpu.VMEM((N_HEADS, DEC_SEQ, 2 * HEAD_DIM), F32),
        ],
        compiler_params=_params("parallel", "arbitrary"),
        name="fox_sample",
    )(q, cache_k, cache_v, k_new, v_new, c_sample, c_sample)


def kernel(x_prompt, x_sample, cache_k, cache_v, cache_logf, ffn1_norm, ffn1_w_gate, ffn1_w_up,
           ffn1_w_down, mix_norm, ffn2_norm, ffn2_w_gate, ffn2_w_up, ffn2_w_down, gmlp_w_in,
           gmlp_ln_g, gmlp_ln_b, gmlp_w_s, gmlp_b_s, gmlp_w_out, kv_norm, w_k, w_v, w_f, b_f,
           fox_w_q, fox_w_o, final_norm):
    bf = lambda w: w.astype(BF16)
    x = jnp.concatenate([x_prompt.reshape(SEQ, D_MODEL), x_sample.reshape(N_SAMPLE, D_MODEL)], axis=0)

    ffn1 = (ffn1_norm, bf(ffn1_w_gate), bf(ffn1_w_up), bf(ffn1_w_down))
    ffn2 = (ffn2_norm, bf(ffn2_w_gate), bf(ffn2_w_up), bf(ffn2_w_down))
    final_g = final_norm.reshape(1, D_MODEL)
    no_bias_model = jnp.zeros((1, D_MODEL), F32)

    reps = GMLP_CHUNK // DEC_SEQ
    ws2 = jnp.stack([gmlp_w_s, jnp.tile(gmlp_w_s[:, :, :DEC_SEQ, :DEC_SEQ], (1, 1, reps, reps))], axis=1)
    bs2 = jnp.stack([gmlp_b_s, jnp.tile(gmlp_b_s[:, :, :DEC_SEQ], (1, 1, reps))], axis=1)[..., None]

    gmlp_v = []
    k_all = v_all = logf_all = None
    for l in range(DEPTH):
        if l == N_A_LAYERS:
            kvg = kv_norm.reshape(1, D_MODEL)
            k_all, k_bf = _norm_matmul(x, kvg, bf(w_k), no_bias_model, 1024, None, (F32, BF16))
            v_all, v_bf = _norm_matmul(x, kvg, bf(w_v), no_bias_model, 1024, None, (F32, BF16))
            w_f_pad = jnp.pad(bf(w_f), ((0, 0), (0, LANES - N_HEADS)))
            b_f_pad = jnp.pad(b_f, (0, LANES - N_HEADS)).reshape(1, LANES)
            (logf_pad,) = _norm_matmul(x, kvg, w_f_pad, b_f_pad, LANES, "log_sigmoid", (F32,))
            logf_all = logf_pad[:, :N_HEADS]
            c_prompt = _cumsum_lanes(logf_all[:SEQ].T)
            c_prompt = c_prompt.reshape(N_HEADS, SEQ // ATT_TILE, 1, ATT_TILE)
            lf_new = logf_all[SEQ:].reshape(DEC_BATCH, DEC_SEQ, N_HEADS).transpose(0, 2, 1)
            lf_cache = cache_logf.transpose(0, 2, 1)
            lf = jnp.concatenate(
                [lf_cache, lf_new, jnp.zeros((DEC_BATCH, N_HEADS, LANES - DEC_SEQ), F32)], axis=-1)
            c_sample = _cumsum_lanes(lf.reshape(DEC_BATCH * N_HEADS, PAST_LEN + LANES))
            c_sample = c_sample.reshape(DEC_BATCH, N_HEADS, PAST_LEN + LANES)

        x = _half_ffn(x, *ffn1, l, final_g, False)
        mg = mix_norm[l].reshape(1, D_MODEL)
        if l < N_A_LAYERS:
            (z,) = _norm_matmul(x, mg, bf(gmlp_w_in[l]), jnp.zeros((1, 2 * D_GATE), F32), 1024, "gelu", (F32,))
            x, vn = _gmlp_mix(z, x, gmlp_ln_g[l].reshape(1, D_GATE), gmlp_ln_b[l].reshape(1, D_GATE),
                              ws2[l], bs2[l], bf(gmlp_w_out[l]))
            gmlp_v.append(vn.reshape(DEC_BATCH, DEC_SEQ, D_GATE))
        else:
            jj = l - N_A_LAYERS
            (q,) = _norm_matmul(x, mg, bf(fox_w_q[jj]), no_bias_model, 1024, None, (BF16,),
                                out_scale=HEAD_DIM ** -0.5 * LOG2E)
            o_prompt = _fox_prompt(q, k_bf, v_bf, c_prompt)
            o_sample = _fox_sample(q, k_bf, v_bf, cache_k, cache_v, c_sample)
            o = jnp.concatenate([o_prompt, o_sample], axis=0)
            x = _matmul_residual(o, bf(fox_w_o[jj]), x, 1024)
        x = _half_ffn(x, *ffn2, l, final_g, l == DEPTH - 1)

    hd = (N_HEADS, HEAD_DIM)
    return (x[:SEQ].reshape(1, SEQ, D_MODEL),
            x[SEQ:].reshape(DEC_BATCH, DEC_SEQ, D_MODEL),
            k_all[:SEQ].reshape(1, SEQ, *hd),
            v_all[:SEQ].reshape(1, SEQ, *hd),
            logf_all[:SEQ].reshape(1, SEQ, N_HEADS),
            k_all[SEQ:].reshape(DEC_BATCH, DEC_SEQ, *hd),
            v_all[SEQ:].reshape(DEC_BATCH, DEC_SEQ, *hd),
            logf_all[SEQ:].reshape(DEC_BATCH, DEC_SEQ, N_HEADS),
            jnp.stack(gmlp_v, axis=0))
```

```python
import functools
import math

import jax
import jax.numpy as jnp
from jax import lax
from jax.experimental import pallas as pl
from jax.experimental.pallas import tpu as pltpu

D_MODEL = 2048
SEQ = 8192
DEPTH = 4
DEC_BATCH = 16
DEC_SEQ = 16
PAST_LEN = 2048
N_A_LAYERS = DEPTH // 2
D_FF = 5632
GMLP_CHUNK = 128
D_GATE = 2 * D_MODEL
GMLP_GROUPS = 4
GROUP_W = D_GATE // GMLP_GROUPS
N_HEADS = 16
HEAD_DIM = D_MODEL // N_HEADS
RMS_EPS = 1e-6
LN_EPS = 1e-5
NEG_INF = -1e30
LOG2E = math.log2(math.e)

N_SAMPLE = DEC_BATCH * DEC_SEQ
M_ROWS = SEQ + N_SAMPLE
LANES = 128

F32 = jnp.float32
BF16 = jnp.bfloat16

ROW_TILE = 768
FFN_ROW_TILE = 1056
FF_TILE = 256
GMLP_ROW_TILE = 256
ATT_TILE = 512
HEADS_PER_STEP = 8
CACHE_TILE = 512
VMEM_LIMIT = 56 * 1024 * 1024


def _params(*sem):
    return pltpu.CompilerParams(dimension_semantics=sem, vmem_limit_bytes=VMEM_LIMIT)


def _rms(xf, g):
    y = xf * lax.rsqrt(jnp.mean(xf * xf, axis=-1, keepdims=True) + RMS_EPS)
    return y * g


def _ffn_kernel(x_ref, g_ref, wg_ref, wu_ref, wd_ref, gf_ref, o_ref, *rest, final_norm):
    h_ref = rest[-1]
    j = pl.program_id(1)

    @pl.when(j == 0)
    def _():
        h_ref[...] = _rms(x_ref[...], g_ref[...]).astype(BF16)
        o_ref[...] = jnp.zeros_like(o_ref)

    h = h_ref[...]
    a = jnp.dot(h, wg_ref[...].astype(BF16), preferred_element_type=F32)
    b = jnp.dot(h, wu_ref[...].astype(BF16), preferred_element_type=F32)
    act = (a * jax.nn.sigmoid(a) * b).astype(BF16)
    o_ref[...] += jnp.dot(act, wd_ref[...].astype(BF16), preferred_element_type=F32)

    @pl.when(j == pl.num_programs(1) - 1)
    def _():
        r = x_ref[...] + 0.5 * o_ref[...]
        if final_norm:
            r = _rms(r, gf_ref[...])
        o_ref[...] = r
        if final_norm:
            @pl.when(pl.program_id(0) == pl.num_programs(0) - 1)
            def _():
                rest[0][...] = r[FFN_ROW_TILE - N_SAMPLE:, :]


def _half_ffn(x, g, wg, wu, wd, layer, gf, final_norm):
    m = x.shape[0]
    grid = (m // FFN_ROW_TILE, D_FF // FF_TILE)
    row_spec = pl.BlockSpec((FFN_ROW_TILE, D_MODEL), lambda i, j: (i, 0), pipeline_mode=pl.Buffered(1))
    if final_norm:
        out_specs = [row_spec, pl.BlockSpec((N_SAMPLE, D_MODEL), lambda i, j: (0, 0))]
        out_shape = [jax.ShapeDtypeStruct((SEQ, D_MODEL), F32), jax.ShapeDtypeStruct((N_SAMPLE, D_MODEL), F32)]
    else:
        out_specs, out_shape = row_spec, jax.ShapeDtypeStruct((m, D_MODEL), F32)
    return pl.pallas_call(
        functools.partial(_ffn_kernel, final_norm=final_norm),
        grid=grid,
        in_specs=[
            pl.BlockSpec((FFN_ROW_TILE, D_MODEL), lambda i, j: (i, 0)),
            pl.BlockSpec((1, D_MODEL), lambda i, j: (0, 0)),
            pl.BlockSpec((None, D_MODEL, FF_TILE), lambda i, j: (layer, 0, j)),
            pl.BlockSpec((None, D_MODEL, FF_TILE), lambda i, j: (layer, 0, j)),
            pl.BlockSpec((None, FF_TILE, D_MODEL), lambda i, j: (layer, j, 0)),
            pl.BlockSpec((1, D_MODEL), lambda i, j: (0, 0)),
        ],
        out_specs=out_specs,
        out_shape=out_shape,
        scratch_shapes=[pltpu.VMEM((FFN_ROW_TILE, D_MODEL), BF16)],
        compiler_params=_params("arbitrary" if final_norm else "parallel", "arbitrary"),
        name="half_ffn",
    )(x, g[layer].reshape(1, D_MODEL), wg, wu, wd, gf)


def _gelu(z):
    return 0.5 * z * (1.0 + lax.erf(z * (1.0 / math.sqrt(2.0))))


def _log_sigmoid(z):
    return -(jnp.maximum(-z, 0.0) + jnp.log1p(jnp.exp(-jnp.abs(z))))


def _norm_matmul_kernel(x_ref, g_ref, w_ref, b_ref, *rest, act, n_out, out_scale, split):
    outs, h_ref = rest[:n_out], rest[n_out]
    j = pl.program_id(1)

    @pl.when(j == 0)
    def _():
        h_ref[...] = _rms(x_ref[...], g_ref[...]).astype(BF16)

    z = jnp.dot(h_ref[...], w_ref[...].astype(BF16), preferred_element_type=F32)
    if act == "gelu":
        z = _gelu(z)
    elif act == "log_sigmoid":
        z = _log_sigmoid(z + b_ref[...])
    if out_scale != 1.0:
        z = z * out_scale
    if split:
        outs[0][...] = z

        @pl.when(pl.program_id(0) == pl.num_programs(0) - 1)
        def _():
            outs[1][...] = z[ROW_TILE - N_SAMPLE:, :]
        outs = outs[2:]
    for o in outs:
        o[...] = z.astype(o.dtype)


def _norm_matmul(x, g, w, b, col_tile, act, out_dtypes, out_scale=1.0, split=False):
    m = x.shape[0]
    n = w.shape[1]
    grid = (m // ROW_TILE, n // col_tile)
    out_spec = pl.BlockSpec((ROW_TILE, col_tile), lambda i, j: (i, j))
    out_specs = [out_spec] * len(out_dtypes)
    out_shape = [jax.ShapeDtypeStruct((m, n), d) for d in out_dtypes]
    if split:
        last = grid[0] - 1
        sample_spec = pl.BlockSpec((N_SAMPLE, col_tile), lambda i, j: (0, jnp.where(i == last, j, 0)))
        out_specs = [out_spec, sample_spec] + out_specs
        out_shape = [jax.ShapeDtypeStruct((SEQ, n), F32), jax.ShapeDtypeStruct((N_SAMPLE, n), F32)] + out_shape
    return pl.pallas_call(
        functools.partial(_norm_matmul_kernel, act=act, n_out=len(out_shape), out_scale=out_scale,
                          split=split),
        grid=grid,
        in_specs=[
            pl.BlockSpec((ROW_TILE, D_MODEL), lambda i, j: (i, 0)),
            pl.BlockSpec((1, D_MODEL), lambda i, j: (0, 0)),
            pl.BlockSpec((D_MODEL, col_tile), lambda i, j: (0, j)),
            pl.BlockSpec((1, col_tile), lambda i, j: (0, j)),
        ],
        out_specs=out_specs,
        out_shape=out_shape,
        scratch_shapes=[pltpu.VMEM((ROW_TILE, D_MODEL), BF16)],
        compiler_params=_params("arbitrary" if split else "parallel", "arbitrary"),
        name="norm_matmul_" + str(act),
    )(x, g, w, b)


def _matmul_residual_kernel(a_ref, w_ref, x_ref, o_ref):
    o_ref[...] = x_ref[...] + jnp.dot(a_ref[...], w_ref[...].astype(BF16), preferred_element_type=F32)


def _matmul_residual(a, w, x, col_tile):
    m, k = a.shape
    n = w.shape[1]
    return pl.pallas_call(
        _matmul_residual_kernel,
        grid=(m // ROW_TILE, n // col_tile),
        in_specs=[
            pl.BlockSpec((ROW_TILE, k), lambda i, j: (i, 0)),
            pl.BlockSpec((k, col_tile), lambda i, j: (0, j)),
            pl.BlockSpec((ROW_TILE, col_tile), lambda i, j: (i, j)),
        ],
        out_specs=pl.BlockSpec((ROW_TILE, col_tile), lambda i, j: (i, j)),
        out_shape=jax.ShapeDtypeStruct((m, n), F32),
        compiler_params=_params("parallel", "arbitrary"),
        name="matmul_residual",
    )(a, w, x)


def _gmlp_mix_kernel(u_ref, v_ref, x_ref, lng_ref, lnb_ref, ws_ref, bs_ref, wo_ref,
                     o_ref, vn_ref, gated_ref, *, n_prompt_tiles):
    i = pl.program_id(0)
    v = v_ref[...]
    mu = jnp.mean(v, axis=-1, keepdims=True)
    vc = v - mu
    vn = vc * lax.rsqrt(jnp.mean(vc * vc, axis=-1, keepdims=True) + LN_EPS)
    vn = vn * lng_ref[...] + lnb_ref[...]

    is_sample = i >= n_prompt_tiles

    @pl.when(is_sample)
    def _():
        vn_ref[...] = vn

    sel = is_sample.astype(jnp.int32)
    shift = jnp.where(is_sample, int(math.log2(DEC_SEQ)), int(math.log2(GMLP_CHUNK)))
    t = lax.broadcasted_iota(jnp.int32, (GMLP_CHUNK, GMLP_CHUNK), 0)
    s = lax.broadcasted_iota(jnp.int32, (GMLP_CHUNK, GMLP_CHUNK), 1)
    mask = (s <= t) & ((t >> shift) == (s >> shift))
    vnb = vn.astype(BF16)
    for grp in range(GMLP_GROUPS):
        w = jnp.where(mask, ws_ref[sel, grp], 0.0).astype(BF16)
        bias = bs_ref[sel, grp]
        cols = slice(grp * GROUP_W, (grp + 1) * GROUP_W)
        for c in range(GMLP_ROW_TILE // GMLP_CHUNK):
            rows = slice(c * GMLP_CHUNK, (c + 1) * GMLP_CHUNK)
            mixed = jnp.dot(w, vnb[rows, cols], preferred_element_type=F32) + bias
            gated_ref[rows, cols] = (u_ref[rows, cols] * mixed).astype(BF16)
    o_ref[...] = x_ref[...] + jnp.dot(gated_ref[...], wo_ref[...], preferred_element_type=F32)


def _gmlp_mix(z, x, ln_g, ln_b, ws2, bs2, w_out):
    m = x.shape[0]
    n_prompt_tiles = SEQ // GMLP_ROW_TILE
    single = pl.Buffered(1)
    return pl.pallas_call(
        functools.partial(_gmlp_mix_kernel, n_prompt_tiles=n_prompt_tiles),
        grid=(m // GMLP_ROW_TILE,),
        in_specs=[
            pl.BlockSpec((GMLP_ROW_TILE, D_GATE), lambda i: (i, 0)),
            pl.BlockSpec((GMLP_ROW_TILE, D_GATE), lambda i: (i, 1)),
            pl.BlockSpec((GMLP_ROW_TILE, D_MODEL), lambda i: (i, 0)),
            pl.BlockSpec((1, D_GATE), lambda i: (0, 0)),
            pl.BlockSpec((1, D_GATE), lambda i: (0, 0)),
            pl.BlockSpec((2, GMLP_GROUPS, GMLP_CHUNK, GMLP_CHUNK), lambda i: (0, 0, 0, 0)),
            pl.BlockSpec((2, GMLP_GROUPS, GMLP_CHUNK, 1), lambda i: (0, 0, 0, 0)),
            pl.BlockSpec((D_GATE, D_MODEL), lambda i: (0, 0), pipeline_mode=single),
        ],
        out_specs=[
            pl.BlockSpec((GMLP_ROW_TILE, D_MODEL), lambda i: (i, 0)),
            pl.BlockSpec((GMLP_ROW_TILE, D_GATE),
                         lambda i: (jnp.maximum(i - n_prompt_tiles, 0), 0), pipeline_mode=single),
        ],
        out_shape=[
            jax.ShapeDtypeStruct((m, D_MODEL), F32),
            jax.ShapeDtypeStruct((N_SAMPLE, D_GATE), F32),
        ],
        scratch_shapes=[pltpu.VMEM((GMLP_ROW_TILE, D_GATE), BF16)],
        compiler_params=_params("arbitrary"),
        name="gmlp_mix",
    )(z, z, x, ln_g, ln_b, ws2, bs2, w_out)


def _scan_kernel(x_ref, o_ref):
    x = x_ref[...]
    n = x.shape[-1]
    idx = lax.broadcasted_iota(jnp.int32, x.shape, 1)
    step = 1
    while step < n:
        x = x + jnp.where(idx >= step, pltpu.roll(x, step, axis=1), 0.0)
        step *= 2
    o_ref[...] = x * LOG2E


def _cumsum_lanes(x):
    return pl.pallas_call(
        _scan_kernel,
        out_shape=jax.ShapeDtypeStruct(x.shape, F32),
        compiler_params=pltpu.CompilerParams(vmem_limit_bytes=VMEM_LIMIT),
        name="cumsum_lanes",
    )(x)


def _tile_lanes(x, reps):
    return jnp.concatenate([x] * reps, axis=1)


def _softmax_step(logits, v, m_ref, acc_ref):
    reps = logits.shape[1] // LANES
    m_prev = m_ref[...]
    m_new = jnp.maximum(m_prev, jnp.max(logits, axis=1, keepdims=True))
    p = jnp.exp2(logits - _tile_lanes(m_new, reps))
    alpha = jnp.exp2(m_prev - m_new)
    v_ones = jnp.concatenate([v, jnp.ones_like(v)], axis=1)
    pv = jnp.dot(p.astype(BF16), v_ones, preferred_element_type=F32)
    acc_ref[...] = _tile_lanes(alpha, 2) * acc_ref[...] + pv
    m_ref[...] = m_new


def _softmax_result(acc_ref):
    acc = acc_ref[...]
    return acc[:, :HEAD_DIM] / acc[:, HEAD_DIM:]


def _fox_prompt_kernel(q_ref, k_ref, v_ref, c_ref, o_ref, m_ref, acc_ref, cq_ref):
    qi = pl.program_id(1)
    reps = ATT_TILE // LANES
    for hh in range(HEADS_PER_STEP):
        cq_ref[hh] = jnp.broadcast_to(c_ref[hh, qi], (LANES, ATT_TILE)).T
    m_ref[...] = jnp.full_like(m_ref, NEG_INF)
    acc_ref[...] = jnp.zeros_like(acc_ref)

    def block(kj, on_diagonal):
        start = pl.multiple_of(kj * ATT_TILE, ATT_TILE)
        for hh in range(HEADS_PER_STEP):
            cols = slice(hh * HEAD_DIM, (hh + 1) * HEAD_DIM)
            k = k_ref[pl.ds(start, ATT_TILE), cols]
            s = lax.dot_general(q_ref[:, cols], k, (((1,), (1,)), ((), ())), preferred_element_type=F32)
            logits = s + (_tile_lanes(cq_ref[hh], reps) - c_ref[hh, kj])
            if on_diagonal:
                row = lax.broadcasted_iota(jnp.int32, (ATT_TILE, ATT_TILE), 0)
                col = lax.broadcasted_iota(jnp.int32, (ATT_TILE, ATT_TILE), 1)
                logits = jnp.where(row >= col, logits, NEG_INF)
            _softmax_step(logits, v_ref[pl.ds(start, ATT_TILE), cols], m_ref.at[hh], acc_ref.at[hh])

    def body(kj, carry):
        block(kj, False)
        return carry

    lax.fori_loop(0, qi, body, 0)
    block(qi, True)
    for hh in range(HEADS_PER_STEP):
        cols = slice(hh * HEAD_DIM, (hh + 1) * HEAD_DIM)
        o_ref[:, cols] = _softmax_result(acc_ref.at[hh]).astype(o_ref.dtype)


def _fox_prompt(q, k, v, c_tiles):
    n_tiles = SEQ // ATT_TILE
    width = HEADS_PER_STEP * HEAD_DIM
    return pl.pallas_call(
        _fox_prompt_kernel,
        grid=(N_HEADS // HEADS_PER_STEP, n_tiles),
        in_specs=[
            pl.BlockSpec((ATT_TILE, width), lambda h, i: (i, h)),
            pl.BlockSpec((SEQ, width), lambda h, i: (0, h), pipeline_mode=pl.Buffered(1)),
            pl.BlockSpec((SEQ, width), lambda h, i: (0, h), pipeline_mode=pl.Buffered(1)),
            pl.BlockSpec((HEADS_PER_STEP, n_tiles, 1, ATT_TILE), lambda h, i: (h, 0, 0, 0)),
        ],
        out_specs=pl.BlockSpec((ATT_TILE, width), lambda h, i: (i, h)),
        out_shape=jax.ShapeDtypeStruct((SEQ, D_MODEL), BF16),
        scratch_shapes=[
            pltpu.VMEM((HEADS_PER_STEP, ATT_TILE, LANES), F32),
            pltpu.VMEM((HEADS_PER_STEP, ATT_TILE, 2 * HEAD_DIM), F32),
            pltpu.VMEM((HEADS_PER_STEP, ATT_TILE, LANES), F32),
        ],
        compiler_params=_params("parallel", "arbitrary"),
        name="fox_prompt",
    )(q, k, v, c_tiles)


def _fox_sample_kernel(q_ref, kc_ref, vc_ref, kn_ref, vn_ref, cc_ref, cn_ref, o_ref, m_ref, acc_ref):
    j = pl.program_id(1)

    @pl.when(j == 0)
    def _():
        m_ref[...] = jnp.full_like(m_ref, NEG_INF)
        acc_ref[...] = jnp.zeros_like(acc_ref)

    c_new = cn_ref[...]
    c_pad = jnp.concatenate([c_new, jnp.zeros((LANES - N_HEADS, LANES), F32)], axis=0)
    cq_all = c_pad.T[:DEC_SEQ, :]

    def cq_of(h):
        return jnp.broadcast_to(cq_all[:, h:h + 1], (DEC_SEQ, LANES))

    for h in range(N_HEADS):
        cols = slice(h * HEAD_DIM, (h + 1) * HEAD_DIM)
        head_rows = pl.ds(h, CACHE_TILE, stride=N_HEADS)
        k = kc_ref[head_rows, :].astype(BF16)
        s = lax.dot_general(q_ref[:, cols], k, (((1,), (1,)), ((), ())), preferred_element_type=F32)
        logits = s + (_tile_lanes(cq_of(h), CACHE_TILE // LANES) - cc_ref[h:h + 1, :])
        _softmax_step(logits, vc_ref[head_rows, :].astype(BF16), m_ref.at[h], acc_ref.at[h])

    @pl.when(j == pl.num_programs(1) - 1)
    def _():
        row = lax.broadcasted_iota(jnp.int32, (DEC_SEQ, LANES), 0)
        col = lax.broadcasted_iota(jnp.int32, (DEC_SEQ, LANES), 1)
        pad = jnp.zeros((LANES - DEC_SEQ, HEAD_DIM), BF16)
        for h in range(N_HEADS):
            cols = slice(h * HEAD_DIM, (h + 1) * HEAD_DIM)
            k = jnp.concatenate([kn_ref[:, cols], pad], axis=0)
            v = jnp.concatenate([vn_ref[:, cols], pad], axis=0)
            s = lax.dot_general(q_ref[:, cols], k, (((1,), (1,)), ((), ())), preferred_element_type=F32)
            logits = s + (cq_of(h) - c_new[h:h + 1, :])
            logits = jnp.where(row >= col, logits, NEG_INF)
            _softmax_step(logits, v, m_ref.at[h], acc_ref.at[h])
            o_ref[:, cols] = _softmax_result(acc_ref.at[h]).astype(o_ref.dtype)


def _fox_sample(q, k_new, v_new, cache_k, cache_v, c_sample):
    first = SEQ // DEC_SEQ
    n_steps = PAST_LEN // CACHE_TILE
    new_spec = pl.BlockSpec((DEC_SEQ, D_MODEL), lambda b, j: (first + b, 0))
    cache_spec = pl.BlockSpec((None, CACHE_TILE * N_HEADS, HEAD_DIM), lambda b, j: (b, j, 0))
    cache_k = cache_k.reshape(DEC_BATCH, PAST_LEN * N_HEADS, HEAD_DIM)
    cache_v = cache_v.reshape(DEC_BATCH, PAST_LEN * N_HEADS, HEAD_DIM)
    return pl.pallas_call(
        _fox_sample_kernel,
        grid=(DEC_BATCH, n_steps),
        in_specs=[
            new_spec, cache_spec, cache_spec, new_spec, new_spec,
            pl.BlockSpec((None, N_HEADS, CACHE_TILE), lambda b, j: (b, 0, j)),
            pl.BlockSpec((None, N_HEADS, LANES), lambda b, j: (b, 0, PAST_LEN // LANES)),
        ],
        out_specs=pl.BlockSpec((DEC_SEQ, D_MODEL), lambda b, j: (b, 0)),
        out_shape=jax.ShapeDtypeStruct((N_SAMPLE, D_MODEL), BF16),
        scratch_shapes=[
            pltpu.VMEM((N_HEADS, DEC_SEQ, LANES), F32),
            pltpu.VMEM((N_HEADS, DEC_SEQ, 2 * HEAD_DIM), F32),
        ],
        compiler_params=_params("parallel", "arbitrary"),
        name="fox_sample",
    )(q, cache_k, cache_v, k_new, v_new, c_sample, c_sample)


def kernel(x_prompt, x_sample, cache_k, cache_v, cache_logf, ffn1_norm, ffn1_w_gate, ffn1_w_up,
           ffn1_w_down, mix_norm, ffn2_norm, ffn2_w_gate, ffn2_w_up, ffn2_w_down, gmlp_w_in,
           gmlp_ln_g, gmlp_ln_b, gmlp_w_s, gmlp_b_s, gmlp_w_out, kv_norm, w_k, w_v, w_f, b_f,
           fox_w_q, fox_w_o, final_norm):
    bf = lambda w: w.astype(BF16)
    x = jnp.concatenate([x_prompt.reshape(SEQ, D_MODEL), x_sample.reshape(N_SAMPLE, D_MODEL)], axis=0)

    ffn1 = (ffn1_norm, ffn1_w_gate, ffn1_w_up, ffn1_w_down)
    ffn2 = (ffn2_norm, ffn2_w_gate, ffn2_w_up, ffn2_w_down)
    final_g = final_norm.reshape(1, D_MODEL)
    no_bias_model = jnp.zeros((1, D_MODEL), F32)

    reps = GMLP_CHUNK // DEC_SEQ
    ws2 = jnp.stack([gmlp_w_s, jnp.tile(gmlp_w_s[:, :, :DEC_SEQ, :DEC_SEQ], (1, 1, reps, reps))], axis=1)
    bs2 = jnp.stack([gmlp_b_s, jnp.tile(gmlp_b_s[:, :, :DEC_SEQ], (1, 1, reps))], axis=1)[..., None]

    gmlp_v = []
    logf_all = None
    for l in range(DEPTH):
        if l == N_A_LAYERS:
            kvg = kv_norm.reshape(1, D_MODEL)
            k_prompt, k_sample, k_bf = _norm_matmul(x, kvg, w_k, no_bias_model, 1024, None, (BF16,), split=True)
            v_prompt, v_sample, v_bf = _norm_matmul(x, kvg, w_v, no_bias_model, 1024, None, (BF16,), split=True)
            w_f_pad = jnp.pad(w_f, ((0, 0), (0, LANES - N_HEADS)))
            b_f_pad = jnp.pad(b_f, (0, LANES - N_HEADS)).reshape(1, LANES)
            (logf_pad,) = _norm_matmul(x, kvg, w_f_pad, b_f_pad, LANES, "log_sigmoid", (F32,))
            logf_all = logf_pad[:, :N_HEADS]
            c_prompt = _cumsum_lanes(logf_all[:SEQ].T)
            c_prompt = c_prompt.reshape(N_HEADS, SEQ // ATT_TILE, 1, ATT_TILE)
            lf_new = logf_all[SEQ:].reshape(DEC_BATCH, DEC_SEQ, N_HEADS).transpose(0, 2, 1)
            lf_cache = cache_logf.transpose(0, 2, 1)
            lf = jnp.concatenate(
                [lf_cache, lf_new, jnp.zeros((DEC_BATCH, N_HEADS, LANES - DEC_SEQ), F32)], axis=-1)
            c_sample = _cumsum_lanes(lf.reshape(DEC_BATCH * N_HEADS, PAST_LEN + LANES))
            c_sample = c_sample.reshape(DEC_BATCH, N_HEADS, PAST_LEN + LANES)

        x = _half_ffn(x, *ffn1, l, final_g, False)
        mg = mix_norm[l].reshape(1, D_MODEL)
        if l < N_A_LAYERS:
            (z,) = _norm_matmul(x, mg, gmlp_w_in[l], jnp.zeros((1, 2 * D_GATE), F32), 1024, "gelu", (F32,))
            x, vn = _gmlp_mix(z, x, gmlp_ln_g[l].reshape(1, D_GATE), gmlp_ln_b[l].reshape(1, D_GATE),
                              ws2[l], bs2[l], bf(gmlp_w_out[l]))
            gmlp_v.append(vn.reshape(DEC_BATCH, DEC_SEQ, D_GATE))
        else:
            jj = l - N_A_LAYERS
            (q,) = _norm_matmul(x, mg, fox_w_q[jj], no_bias_model, 1024, None, (BF16,),
                                out_scale=HEAD_DIM ** -0.5 * LOG2E)
            o_prompt = _fox_prompt(q, k_bf, v_bf, c_prompt)
            o_sample = _fox_sample(q, k_bf, v_bf, cache_k, cache_v, c_sample)
            o = jnp.concatenate([o_prompt, o_sample], axis=0)
            x = _matmul_residual(o, fox_w_o[jj], x, 1024)
        x = _half_ffn(x, *ffn2, l, final_g, l == DEPTH - 1)

    y_prompt, y_sample = x
    hd = (N_HEADS, HEAD_DIM)
    return (y_prompt.reshape(1, SEQ, D_MODEL),
            y_sample.reshape(DEC_BATCH, DEC_SEQ, D_MODEL),
            k_prompt.reshape(1, SEQ, *hd),
            v_prompt.reshape(1, SEQ, *hd),
            logf_all[:SEQ].reshape(1, SEQ, N_HEADS),
            k_sample.reshape(DEC_BATCH, DEC_SEQ, *hd),
            v_sample.reshape(DEC_BATCH, DEC_SEQ, *hd),
            logf_all[SEQ:].reshape(DEC_BATCH, DEC_SEQ, N_HEADS),
            jnp.stack(gmlp_v, axis=0))
```

```python
import functools
import math

import jax
import jax.numpy as jnp
from jax import lax
from jax.experimental import pallas as pl
from jax.experimental.pallas import tpu as pltpu

D_MODEL = 2048
SEQ = 8192
DEPTH = 4
DEC_BATCH = 16
DEC_SEQ = 16
PAST_LEN = 2048
N_A_LAYERS = DEPTH // 2
D_FF = 5632
GMLP_CHUNK = 128
D_GATE = 2 * D_MODEL
GMLP_GROUPS = 4
GROUP_W = D_GATE // GMLP_GROUPS
N_HEADS = 16
HEAD_DIM = D_MODEL // N_HEADS
RMS_EPS = 1e-6
LN_EPS = 1e-5
NEG_INF = -1e30
LOG2E = math.log2(math.e)

N_SAMPLE = DEC_BATCH * DEC_SEQ
M_ROWS = SEQ + N_SAMPLE
LANES = 128

F32 = jnp.float32
BF16 = jnp.bfloat16

ROW_TILE = 768
SMALL_ROW_TILE = 384
FFN_ROW_TILE = 1056
FF_TILE = 256
GMLP_ROW_TILE = 256
ATT_TILE = 512
HEADS_PER_STEP = 8
CACHE_TILE = 512
VMEM_LIMIT = 56 * 1024 * 1024


def _params(*sem):
    return pltpu.CompilerParams(dimension_semantics=sem, vmem_limit_bytes=VMEM_LIMIT)


def _rms(xf, g):
    y = xf * lax.rsqrt(jnp.mean(xf * xf, axis=-1, keepdims=True) + RMS_EPS)
    return y * g


def _ffn_kernel(x_ref, g_ref, wg_ref, wu_ref, wd_ref, gf_ref, o_ref, *rest, final_norm):
    h_ref = rest[-1]
    j = pl.program_id(1)

    @pl.when(j == 0)
    def _():
        h_ref[...] = _rms(x_ref[...], g_ref[...]).astype(BF16)
        o_ref[...] = jnp.zeros_like(o_ref)

    h = h_ref[...]
    a = jnp.dot(h, wg_ref[...].astype(BF16), preferred_element_type=F32)
    b = jnp.dot(h, wu_ref[...].astype(BF16), preferred_element_type=F32)
    act = (a * jax.nn.sigmoid(a) * b).astype(BF16)
    o_ref[...] += jnp.dot(act, wd_ref[...].astype(BF16), preferred_element_type=F32)

    @pl.when(j == pl.num_programs(1) - 1)
    def _():
        r = x_ref[...] + 0.5 * o_ref[...]
        if final_norm:
            r = _rms(r, gf_ref[...])
        o_ref[...] = r
        if final_norm:
            @pl.when(pl.program_id(0) == pl.num_programs(0) - 1)
            def _():
                rest[0][...] = r[FFN_ROW_TILE - N_SAMPLE:, :]


def _half_ffn(x, g, wg, wu, wd, layer, gf, final_norm):
    m = x.shape[0]
    grid = (m // FFN_ROW_TILE, D_FF // FF_TILE)
    row_spec = pl.BlockSpec((FFN_ROW_TILE, D_MODEL), lambda i, j: (i, 0), pipeline_mode=pl.Buffered(1))
    if final_norm:
        out_specs = [row_spec, pl.BlockSpec((N_SAMPLE, D_MODEL), lambda i, j: (0, 0))]
        out_shape = [jax.ShapeDtypeStruct((SEQ, D_MODEL), F32), jax.ShapeDtypeStruct((N_SAMPLE, D_MODEL), F32)]
    else:
        out_specs, out_shape = row_spec, jax.ShapeDtypeStruct((m, D_MODEL), F32)
    return pl.pallas_call(
        functools.partial(_ffn_kernel, final_norm=final_norm),
        grid=grid,
        in_specs=[
            pl.BlockSpec((FFN_ROW_TILE, D_MODEL), lambda i, j: (i, 0)),
            pl.BlockSpec((1, D_MODEL), lambda i, j: (0, 0)),
            pl.BlockSpec((None, D_MODEL, FF_TILE), lambda i, j: (layer, 0, j)),
            pl.BlockSpec((None, D_MODEL, FF_TILE), lambda i, j: (layer, 0, j)),
            pl.BlockSpec((None, FF_TILE, D_MODEL), lambda i, j: (layer, j, 0)),
            pl.BlockSpec((1, D_MODEL), lambda i, j: (0, 0)),
        ],
        out_specs=out_specs,
        out_shape=out_shape,
        scratch_shapes=[pltpu.VMEM((FFN_ROW_TILE, D_MODEL), BF16)],
        compiler_params=_params("arbitrary" if final_norm else "parallel", "arbitrary"),
        name="half_ffn",
    )(x, g[layer].reshape(1, D_MODEL), wg, wu, wd, gf)


def _gelu(z):
    return 0.5 * z * (1.0 + lax.erf(z * (1.0 / math.sqrt(2.0))))


def _log_sigmoid(z):
    return -(jnp.maximum(-z, 0.0) + jnp.log1p(jnp.exp(-jnp.abs(z))))


def _norm_matmul_kernel(x_ref, g_ref, w_ref, b_ref, *rest, act, n_out, out_scale, split, row_tile):
    outs, h_ref = rest[:n_out], rest[n_out]
    j = pl.program_id(1)

    @pl.when(j == 0)
    def _():
        h_ref[...] = _rms(x_ref[...], g_ref[...]).astype(BF16)

    z = jnp.dot(h_ref[...], w_ref[...].astype(BF16), preferred_element_type=F32)
    if act == "gelu":
        z = _gelu(z)
    elif act == "log_sigmoid":
        z = _log_sigmoid(z + b_ref[...])
    if out_scale != 1.0:
        z = z * out_scale
    if split:
        outs[0][...] = z

        @pl.when(pl.program_id(0) == pl.num_programs(0) - 1)
        def _():
            outs[1][...] = z[row_tile - N_SAMPLE:, :]
        outs = outs[2:]
    for o in outs:
        o[...] = z.astype(o.dtype)


def _norm_matmul(x, g, w, layer, b, row_tile, col_tile, act, out_dtypes, out_scale=1.0, split=False):
    m = x.shape[0]
    n = w.shape[2]
    grid = (m // row_tile, n // col_tile)
    w_mode = dict(pipeline_mode=pl.Buffered(1)) if col_tile == n else {}
    out_spec = pl.BlockSpec((row_tile, col_tile), lambda i, j: (i, j))
    out_specs = [out_spec] * len(out_dtypes)
    out_shape = [jax.ShapeDtypeStruct((m, n), d) for d in out_dtypes]
    if split:
        last = grid[0] - 1
        sample_spec = pl.BlockSpec((N_SAMPLE, col_tile), lambda i, j: (0, jnp.where(i == last, j, 0)))
        out_specs = [out_spec, sample_spec] + out_specs
        out_shape = [jax.ShapeDtypeStruct((SEQ, n), F32), jax.ShapeDtypeStruct((N_SAMPLE, n), F32)] + out_shape
    return pl.pallas_call(
        functools.partial(_norm_matmul_kernel, act=act, n_out=len(out_shape), out_scale=out_scale,
                          split=split, row_tile=row_tile),
        grid=grid,
        in_specs=[
            pl.BlockSpec((row_tile, D_MODEL), lambda i, j: (i, 0)),
            pl.BlockSpec((1, D_MODEL), lambda i, j: (0, 0)),
            pl.BlockSpec((None, D_MODEL, col_tile), lambda i, j: (layer, 0, j), **w_mode),
            pl.BlockSpec((1, col_tile), lambda i, j: (0, j)),
        ],
        out_specs=out_specs,
        out_shape=out_shape,
        scratch_shapes=[pltpu.VMEM((row_tile, D_MODEL), BF16)],
        compiler_params=_params("arbitrary" if split else "parallel", "arbitrary"),
        name="norm_matmul_" + str(act),
    )(x, g, w, b)


def _matmul_residual_kernel(a_ref, w_ref, x_ref, o_ref):
    o_ref[...] = x_ref[...] + jnp.dot(a_ref[...], w_ref[...].astype(BF16), preferred_element_type=F32)


def _matmul_residual(a, w, layer, x):
    m, k = a.shape
    n = w.shape[2]
    return pl.pallas_call(
        _matmul_residual_kernel,
        grid=(m // SMALL_ROW_TILE,),
        in_specs=[
            pl.BlockSpec((SMALL_ROW_TILE, k), lambda i: (i, 0)),
            pl.BlockSpec((None, k, n), lambda i: (layer, 0, 0), pipeline_mode=pl.Buffered(1)),
            pl.BlockSpec((SMALL_ROW_TILE, n), lambda i: (i, 0)),
        ],
        out_specs=pl.BlockSpec((SMALL_ROW_TILE, n), lambda i: (i, 0)),
        out_shape=jax.ShapeDtypeStruct((m, n), F32),
        compiler_params=_params("parallel"),
        name="matmul_residual",
    )(a, w, x)


def _gmlp_mix_kernel(u_ref, v_ref, x_ref, lng_ref, lnb_ref, ws_ref, bs_ref, wo_ref,
                     o_ref, vn_ref, gated_ref, *, n_prompt_tiles):
    i = pl.program_id(0)
    v = v_ref[...]
    mu = jnp.mean(v, axis=-1, keepdims=True)
    vc = v - mu
    vn = vc * lax.rsqrt(jnp.mean(vc * vc, axis=-1, keepdims=True) + LN_EPS)
    vn = vn * lng_ref[...] + lnb_ref[...]

    is_sample = i >= n_prompt_tiles

    @pl.when(is_sample)
    def _():
        vn_ref[...] = vn

    sel = is_sample.astype(jnp.int32)
    shift = jnp.where(is_sample, int(math.log2(DEC_SEQ)), int(math.log2(GMLP_CHUNK)))
    t = lax.broadcasted_iota(jnp.int32, (GMLP_CHUNK, GMLP_CHUNK), 0)
    s = lax.broadcasted_iota(jnp.int32, (GMLP_CHUNK, GMLP_CHUNK), 1)
    mask = (s <= t) & ((t >> shift) == (s >> shift))
    vnb = vn.astype(BF16)
    for grp in range(GMLP_GROUPS):
        w = jnp.where(mask, ws_ref[sel, grp], 0.0).astype(BF16)
        bias = bs_ref[sel, grp]
        cols = slice(grp * GROUP_W, (grp + 1) * GROUP_W)
        for c in range(GMLP_ROW_TILE // GMLP_CHUNK):
            rows = slice(c * GMLP_CHUNK, (c + 1) * GMLP_CHUNK)
            mixed = jnp.dot(w, vnb[rows, cols], preferred_element_type=F32) + bias
            gated_ref[rows, cols] = (u_ref[rows, cols] * mixed).astype(BF16)
    o_ref[...] = x_ref[...] + jnp.dot(gated_ref[...], wo_ref[...], preferred_element_type=F32)


def _gmlp_mix(z, x, ln_g, ln_b, ws2, bs2, w_out):
    m = x.shape[0]
    n_prompt_tiles = SEQ // GMLP_ROW_TILE
    single = pl.Buffered(1)
    return pl.pallas_call(
        functools.partial(_gmlp_mix_kernel, n_prompt_tiles=n_prompt_tiles),
        grid=(m // GMLP_ROW_TILE,),
        in_specs=[
            pl.BlockSpec((GMLP_ROW_TILE, D_GATE), lambda i: (i, 0)),
            pl.BlockSpec((GMLP_ROW_TILE, D_GATE), lambda i: (i, 1)),
            pl.BlockSpec((GMLP_ROW_TILE, D_MODEL), lambda i: (i, 0)),
            pl.BlockSpec((1, D_GATE), lambda i: (0, 0)),
            pl.BlockSpec((1, D_GATE), lambda i: (0, 0)),
            pl.BlockSpec((2, GMLP_GROUPS, GMLP_CHUNK, GMLP_CHUNK), lambda i: (0, 0, 0, 0)),
            pl.BlockSpec((2, GMLP_GROUPS, GMLP_CHUNK, 1), lambda i: (0, 0, 0, 0)),
            pl.BlockSpec((D_GATE, D_MODEL), lambda i: (0, 0), pipeline_mode=single),
        ],
        out_specs=[
            pl.BlockSpec((GMLP_ROW_TILE, D_MODEL), lambda i: (i, 0)),
            pl.BlockSpec((GMLP_ROW_TILE, D_GATE),
                         lambda i: (jnp.maximum(i - n_prompt_tiles, 0), 0), pipeline_mode=single),
        ],
        out_shape=[
            jax.ShapeDtypeStruct((m, D_MODEL), F32),
            jax.ShapeDtypeStruct((N_SAMPLE, D_GATE), F32),
        ],
        scratch_shapes=[pltpu.VMEM((GMLP_ROW_TILE, D_GATE), BF16)],
        compiler_params=_params("arbitrary"),
        name="gmlp_mix",
    )(z, z, x, ln_g, ln_b, ws2, bs2, w_out)


def _scan_kernel(x_ref, o_ref):
    x = x_ref[...]
    n = x.shape[-1]
    idx = lax.broadcasted_iota(jnp.int32, x.shape, 1)
    step = 1
    while step < n:
        x = x + jnp.where(idx >= step, pltpu.roll(x, step, axis=1), 0.0)
        step *= 2
    o_ref[...] = x * LOG2E


def _cumsum_lanes(x):
    return pl.pallas_call(
        _scan_kernel,
        out_shape=jax.ShapeDtypeStruct(x.shape, F32),
        compiler_params=pltpu.CompilerParams(vmem_limit_bytes=VMEM_LIMIT),
        name="cumsum_lanes",
    )(x)


def _tile_lanes(x, reps):
    return jnp.concatenate([x] * reps, axis=1)


def _softmax_step(logits, v, m_ref, acc_ref):
    reps = logits.shape[1] // LANES
    m_prev = m_ref[...]
    m_new = jnp.maximum(m_prev, jnp.max(logits, axis=1, keepdims=True))
    p = jnp.exp2(logits - _tile_lanes(m_new, reps))
    alpha = jnp.exp2(m_prev - m_new)
    v_ones = jnp.concatenate([v, jnp.ones_like(v)], axis=1)
    pv = jnp.dot(p.astype(BF16), v_ones, preferred_element_type=F32)
    acc_ref[...] = _tile_lanes(alpha, 2) * acc_ref[...] + pv
    m_ref[...] = m_new


def _softmax_result(acc_ref):
    acc = acc_ref[...]
    return acc[:, :HEAD_DIM] / acc[:, HEAD_DIM:]


def _fox_prompt_kernel(q_ref, k_ref, v_ref, c_ref, o_ref, m_ref, acc_ref, cq_ref):
    qi = pl.program_id(1)
    reps = ATT_TILE // LANES
    for hh in range(HEADS_PER_STEP):
        cq_ref[hh] = jnp.broadcast_to(c_ref[hh, qi], (LANES, ATT_TILE)).T
    m_ref[...] = jnp.full_like(m_ref, NEG_INF)
    acc_ref[...] = jnp.zeros_like(acc_ref)

    def block(kj, on_diagonal):
        start = pl.multiple_of(kj * ATT_TILE, ATT_TILE)
        for hh in range(HEADS_PER_STEP):
            cols = slice(hh * HEAD_DIM, (hh + 1) * HEAD_DIM)
            k = k_ref[pl.ds(start, ATT_TILE), cols]
            s = lax.dot_general(q_ref[:, cols], k, (((1,), (1,)), ((), ())), preferred_element_type=F32)
            logits = s + (_tile_lanes(cq_ref[hh], reps) - c_ref[hh, kj])
            if on_diagonal:
                row = lax.broadcasted_iota(jnp.int32, (ATT_TILE, ATT_TILE), 0)
                col = lax.broadcasted_iota(jnp.int32, (ATT_TILE, ATT_TILE), 1)
                logits = jnp.where(row >= col, logits, NEG_INF)
            _softmax_step(logits, v_ref[pl.ds(start, ATT_TILE), cols], m_ref.at[hh], acc_ref.at[hh])

    def body(kj, carry):
        block(kj, False)
        return carry

    lax.fori_loop(0, qi, body, 0)
    block(qi, True)
    for hh in range(HEADS_PER_STEP):
        cols = slice(hh * HEAD_DIM, (hh + 1) * HEAD_DIM)
        o_ref[:, cols] = _softmax_result(acc_ref.at[hh]).astype(o_ref.dtype)


def _fox_prompt(q, k, v, c_tiles):
    n_tiles = SEQ // ATT_TILE
    width = HEADS_PER_STEP * HEAD_DIM
    return pl.pallas_call(
        _fox_prompt_kernel,
        grid=(N_HEADS // HEADS_PER_STEP, n_tiles),
        in_specs=[
            pl.BlockSpec((ATT_TILE, width), lambda h, i: (i, h)),
            pl.BlockSpec((SEQ, width), lambda h, i: (0, h), pipeline_mode=pl.Buffered(1)),
            pl.BlockSpec((SEQ, width), lambda h, i: (0, h), pipeline_mode=pl.Buffered(1)),
            pl.BlockSpec((HEADS_PER_STEP, n_tiles, 1, ATT_TILE), lambda h, i: (h, 0, 0, 0)),
        ],
        out_specs=pl.BlockSpec((ATT_TILE, width), lambda h, i: (i, h)),
        out_shape=jax.ShapeDtypeStruct((SEQ, D_MODEL), BF16),
        scratch_shapes=[
            pltpu.VMEM((HEADS_PER_STEP, ATT_TILE, LANES), F32),
            pltpu.VMEM((HEADS_PER_STEP, ATT_TILE, 2 * HEAD_DIM), F32),
            pltpu.VMEM((HEADS_PER_STEP, ATT_TILE, LANES), F32),
        ],
        compiler_params=_params("parallel", "arbitrary"),
        name="fox_prompt",
    )(q, k, v, c_tiles)


def _fox_sample_kernel(q_ref, kc_ref, vc_ref, kn_ref, vn_ref, cc_ref, cn_ref, o_ref, m_ref, acc_ref):
    j = pl.program_id(1)

    @pl.when(j == 0)
    def _():
        m_ref[...] = jnp.full_like(m_ref, NEG_INF)
        acc_ref[...] = jnp.zeros_like(acc_ref)

    c_new = cn_ref[...]
    c_pad = jnp.concatenate([c_new, jnp.zeros((LANES - N_HEADS, LANES), F32)], axis=0)
    cq_all = c_pad.T[:DEC_SEQ, :]

    def cq_of(h):
        return jnp.broadcast_to(cq_all[:, h:h + 1], (DEC_SEQ, LANES))

    for h in range(N_HEADS):
        cols = slice(h * HEAD_DIM, (h + 1) * HEAD_DIM)
        head_rows = pl.ds(h, CACHE_TILE, stride=N_HEADS)
        k = kc_ref[head_rows, :].astype(BF16)
        s = lax.dot_general(q_ref[:, cols], k, (((1,), (1,)), ((), ())), preferred_element_type=F32)
        logits = s + (_tile_lanes(cq_of(h), CACHE_TILE // LANES) - cc_ref[h:h + 1, :])
        _softmax_step(logits, vc_ref[head_rows, :].astype(BF16), m_ref.at[h], acc_ref.at[h])

    @pl.when(j == pl.num_programs(1) - 1)
    def _():
        row = lax.broadcasted_iota(jnp.int32, (DEC_SEQ, LANES), 0)
        col = lax.broadcasted_iota(jnp.int32, (DEC_SEQ, LANES), 1)
        pad = jnp.zeros((LANES - DEC_SEQ, HEAD_DIM), BF16)
        for h in range(N_HEADS):
            cols = slice(h * HEAD_DIM, (h + 1) * HEAD_DIM)
            k = jnp.concatenate([kn_ref[:, cols], pad], axis=0)
            v = jnp.concatenate([vn_ref[:, cols], pad], axis=0)
            s = lax.dot_general(q_ref[:, cols], k, (((1,), (1,)), ((), ())), preferred_element_type=F32)
            logits = s + (cq_of(h) - c_new[h:h + 1, :])
            logits = jnp.where(row >= col, logits, NEG_INF)
            _softmax_step(logits, v, m_ref.at[h], acc_ref.at[h])
            o_ref[:, cols] = _softmax_result(acc_ref.at[h]).astype(o_ref.dtype)


def _fox_sample(q, k_new, v_new, cache_k, cache_v, c_sample):
    first = SEQ // DEC_SEQ
    n_steps = PAST_LEN // CACHE_TILE
    new_spec = pl.BlockSpec((DEC_SEQ, D_MODEL), lambda b, j: (first + b, 0))
    cache_spec = pl.BlockSpec((None, CACHE_TILE * N_HEADS, HEAD_DIM), lambda b, j: (b, j, 0))
    cache_k = cache_k.reshape(DEC_BATCH, PAST_LEN * N_HEADS, HEAD_DIM)
    cache_v = cache_v.reshape(DEC_BATCH, PAST_LEN * N_HEADS, HEAD_DIM)
    return pl.pallas_call(
        _fox_sample_kernel,
        grid=(DEC_BATCH, n_steps),
        in_specs=[
            new_spec, cache_spec, cache_spec, new_spec, new_spec,
            pl.BlockSpec((None, N_HEADS, CACHE_TILE), lambda b, j: (b, 0, j)),
            pl.BlockSpec((None, N_HEADS, LANES), lambda b, j: (b, 0, PAST_LEN // LANES)),
        ],
        out_specs=pl.BlockSpec((DEC_SEQ, D_MODEL), lambda b, j: (b, 0)),
        out_shape=jax.ShapeDtypeStruct((N_SAMPLE, D_MODEL), BF16),
        scratch_shapes=[
            pltpu.VMEM((N_HEADS, DEC_SEQ, LANES), F32),
            pltpu.VMEM((N_HEADS, DEC_SEQ, 2 * HEAD_DIM), F32),
        ],
        compiler_params=_params("parallel", "arbitrary"),
        name="fox_sample",
    )(q, cache_k, cache_v, k_new, v_new, c_sample, c_sample)


def kernel(x_prompt, x_sample, cache_k, cache_v, cache_logf, ffn1_norm, ffn1_w_gate, ffn1_w_up,
           ffn1_w_down, mix_norm, ffn2_norm, ffn2_w_gate, ffn2_w_up, ffn2_w_down, gmlp_w_in,
           gmlp_ln_g, gmlp_ln_b, gmlp_w_s, gmlp_b_s, gmlp_w_out, kv_norm, w_k, w_v, w_f, b_f,
           fox_w_q, fox_w_o, final_norm):
    bf = lambda w: w.astype(BF16)
    x = jnp.concatenate([x_prompt.reshape(SEQ, D_MODEL), x_sample.reshape(N_SAMPLE, D_MODEL)], axis=0)

    ffn1 = (ffn1_norm, ffn1_w_gate, ffn1_w_up, ffn1_w_down)
    ffn2 = (ffn2_norm, ffn2_w_gate, ffn2_w_up, ffn2_w_down)
    final_g = final_norm.reshape(1, D_MODEL)
    w_in_bf = bf(gmlp_w_in)
    no_bias_model = jnp.zeros((1, D_MODEL), F32)

    reps = GMLP_CHUNK // DEC_SEQ
    ws2 = jnp.stack([gmlp_w_s, jnp.tile(gmlp_w_s[:, :, :DEC_SEQ, :DEC_SEQ], (1, 1, reps, reps))], axis=1)
    bs2 = jnp.stack([gmlp_b_s, jnp.tile(gmlp_b_s[:, :, :DEC_SEQ], (1, 1, reps))], axis=1)[..., None]

    gmlp_v = []
    logf_all = None
    for l in range(DEPTH):
        if l == N_A_LAYERS:
            kvg = kv_norm.reshape(1, D_MODEL)
            k_prompt, k_sample, k_bf = _norm_matmul(x, kvg, w_k[None], 0, no_bias_model, SMALL_ROW_TILE,
                                                    D_MODEL, None, (BF16,), split=True)
            v_prompt, v_sample, v_bf = _norm_matmul(x, kvg, w_v[None], 0, no_bias_model, SMALL_ROW_TILE,
                                                    D_MODEL, None, (BF16,), split=True)
            w_f_pad = jnp.pad(w_f, ((0, 0), (0, LANES - N_HEADS)))[None]
            b_f_pad = jnp.pad(b_f, (0, LANES - N_HEADS)).reshape(1, LANES)
            (logf_pad,) = _norm_matmul(x, kvg, w_f_pad, 0, b_f_pad, ROW_TILE, LANES, "log_sigmoid", (F32,))
            logf_all = logf_pad[:, :N_HEADS]
            c_prompt = _cumsum_lanes(logf_all[:SEQ].T)
            c_prompt = c_prompt.reshape(N_HEADS, SEQ // ATT_TILE, 1, ATT_TILE)
            lf_new = logf_all[SEQ:].reshape(DEC_BATCH, DEC_SEQ, N_HEADS).transpose(0, 2, 1)
            lf_cache = cache_logf.transpose(0, 2, 1)
            lf = jnp.concatenate(
                [lf_cache, lf_new, jnp.zeros((DEC_BATCH, N_HEADS, LANES - DEC_SEQ), F32)], axis=-1)
            c_sample = _cumsum_lanes(lf.reshape(DEC_BATCH * N_HEADS, PAST_LEN + LANES))
            c_sample = c_sample.reshape(DEC_BATCH, N_HEADS, PAST_LEN + LANES)

        x = _half_ffn(x, *ffn1, l, final_g, False)
        mg = mix_norm[l].reshape(1, D_MODEL)
        if l < N_A_LAYERS:
            (z,) = _norm_matmul(x, mg, w_in_bf, l, jnp.zeros((1, 2 * D_GATE), F32), ROW_TILE, 1024, "gelu", (F32,))
            x, vn = _gmlp_mix(z, x, gmlp_ln_g[l].reshape(1, D_GATE), gmlp_ln_b[l].reshape(1, D_GATE),
                              ws2[l], bs2[l], bf(gmlp_w_out[l]))
            gmlp_v.append(vn.reshape(DEC_BATCH, DEC_SEQ, D_GATE))
        else:
            jj = l - N_A_LAYERS
            (q,) = _norm_matmul(x, mg, fox_w_q, jj, no_bias_model, SMALL_ROW_TILE, D_MODEL, None, (BF16,),
                                out_scale=HEAD_DIM ** -0.5 * LOG2E)
            o_prompt = _fox_prompt(q, k_bf, v_bf, c_prompt)
            o_sample = _fox_sample(q, k_bf, v_bf, cache_k, cache_v, c_sample)
            o = jnp.concatenate([o_prompt, o_sample], axis=0)
            x = _matmul_residual(o, fox_w_o, jj, x)
        x = _half_ffn(x, *ffn2, l, final_g, l == DEPTH - 1)

    y_prompt, y_sample = x
    hd = (N_HEADS, HEAD_DIM)
    return (y_prompt.reshape(1, SEQ, D_MODEL),
            y_sample.reshape(DEC_BATCH, DEC_SEQ, D_MODEL),
            k_prompt.reshape(1, SEQ, *hd),
            v_prompt.reshape(1, SEQ, *hd),
            logf_all[:SEQ].reshape(1, SEQ, N_HEADS),
            k_sample.reshape(DEC_BATCH, DEC_SEQ, *hd),
            v_sample.reshape(DEC_BATCH, DEC_SEQ, *hd),
            logf_all[SEQ:].reshape(DEC_BATCH, DEC_SEQ, N_HEADS),
            jnp.stack(gmlp_v, axis=0))
```

```python
import functools
import math

import jax
import jax.numpy as jnp
from jax import lax
from jax.experimental import pallas as pl
from jax.experimental.pallas import tpu as pltpu

D_MODEL = 2048
SEQ = 8192
DEPTH = 4
DEC_BATCH = 16
DEC_SEQ = 16
PAST_LEN = 2048
N_A_LAYERS = DEPTH // 2
D_FF = 5632
GMLP_CHUNK = 128
D_GATE = 2 * D_MODEL
GMLP_GROUPS = 4
GROUP_W = D_GATE // GMLP_GROUPS
N_HEADS = 16
HEAD_DIM = D_MODEL // N_HEADS
RMS_EPS = 1e-6
LN_EPS = 1e-5
NEG_INF = -1e30
LOG2E = math.log2(math.e)

N_SAMPLE = DEC_BATCH * DEC_SEQ
M_ROWS = SEQ + N_SAMPLE
LANES = 128

F32 = jnp.float32
BF16 = jnp.bfloat16

ROW_TILE = 768
SMALL_ROW_TILE = 384
FFN_ROW_TILE = 1056
FF_TILE = 256
GMLP_ROW_TILE = 256
ATT_TILE = 512
ATT_Q_TILE = 512
HEADS_PER_STEP = 8
CACHE_TILE = 512
VMEM_LIMIT = 56 * 1024 * 1024


def _params(*sem):
    return pltpu.CompilerParams(dimension_semantics=sem, vmem_limit_bytes=VMEM_LIMIT)


def _rms(xf, g):
    y = xf * lax.rsqrt(jnp.mean(xf * xf, axis=-1, keepdims=True) + RMS_EPS)
    return y * g


def _ffn_kernel(x_ref, g_ref, wg_ref, wu_ref, wd_ref, gf_ref, o_ref, *rest, final_norm):
    h_ref = rest[-1]
    j = pl.program_id(1)

    @pl.when(j == 0)
    def _():
        h_ref[...] = _rms(x_ref[...], g_ref[...]).astype(BF16)
        o_ref[...] = jnp.zeros_like(o_ref)

    h = h_ref[...]
    a = jnp.dot(h, wg_ref[...].astype(BF16), preferred_element_type=F32)
    b = jnp.dot(h, wu_ref[...].astype(BF16), preferred_element_type=F32)
    act = (a * jax.nn.sigmoid(a) * b).astype(BF16)
    o_ref[...] += jnp.dot(act, wd_ref[...].astype(BF16), preferred_element_type=F32)

    @pl.when(j == pl.num_programs(1) - 1)
    def _():
        r = x_ref[...] + 0.5 * o_ref[...]
        if final_norm:
            r = _rms(r, gf_ref[...])
        o_ref[...] = r
        if final_norm:
            @pl.when(pl.program_id(0) == pl.num_programs(0) - 1)
            def _():
                rest[0][...] = r[FFN_ROW_TILE - N_SAMPLE:, :]


def _half_ffn(x, g, wg, wu, wd, layer, gf, final_norm):
    m = x.shape[0]
    grid = (m // FFN_ROW_TILE, D_FF // FF_TILE)
    row_spec = pl.BlockSpec((FFN_ROW_TILE, D_MODEL), lambda i, j: (i, 0), pipeline_mode=pl.Buffered(1))
    if final_norm:
        out_specs = [row_spec, pl.BlockSpec((N_SAMPLE, D_MODEL), lambda i, j: (0, 0))]
        out_shape = [jax.ShapeDtypeStruct((SEQ, D_MODEL), F32), jax.ShapeDtypeStruct((N_SAMPLE, D_MODEL), F32)]
    else:
        out_specs, out_shape = row_spec, jax.ShapeDtypeStruct((m, D_MODEL), F32)
    return pl.pallas_call(
        functools.partial(_ffn_kernel, final_norm=final_norm),
        grid=grid,
        in_specs=[
            pl.BlockSpec((FFN_ROW_TILE, D_MODEL), lambda i, j: (i, 0)),
            pl.BlockSpec((1, D_MODEL), lambda i, j: (0, 0)),
            pl.BlockSpec((None, D_MODEL, FF_TILE), lambda i, j: (layer, 0, j)),
            pl.BlockSpec((None, D_MODEL, FF_TILE), lambda i, j: (layer, 0, j)),
            pl.BlockSpec((None, FF_TILE, D_MODEL), lambda i, j: (layer, j, 0)),
            pl.BlockSpec((1, D_MODEL), lambda i, j: (0, 0)),
        ],
        out_specs=out_specs,
        out_shape=out_shape,
        scratch_shapes=[pltpu.VMEM((FFN_ROW_TILE, D_MODEL), BF16)],
        compiler_params=_params("arbitrary" if final_norm else "parallel", "arbitrary"),
        name="half_ffn",
    )(x, g[layer].reshape(1, D_MODEL), wg, wu, wd, gf)


def _gelu(z):
    return 0.5 * z * (1.0 + lax.erf(z * (1.0 / math.sqrt(2.0))))


def _log_sigmoid(z):
    return -(jnp.maximum(-z, 0.0) + jnp.log1p(jnp.exp(-jnp.abs(z))))


def _norm_matmul_kernel(x_ref, g_ref, w_ref, *rest, act, n_out, out_scale, split, row_tile, forget):
    if forget:
        wf_ref, bf_ref, rest = rest[0], rest[1], rest[2:]
    outs, h_ref = rest[:n_out], rest[n_out]
    j = pl.program_id(1)

    @pl.when(j == 0)
    def _():
        h_ref[...] = _rms(x_ref[...], g_ref[...]).astype(BF16)

    z = jnp.dot(h_ref[...], w_ref[...].astype(BF16), preferred_element_type=F32)
    if act == "gelu":
        z = _gelu(z)
    if forget:
        zf = jnp.dot(h_ref[...], wf_ref[...].astype(BF16), preferred_element_type=F32)
        outs[-1][...] = _log_sigmoid(zf + bf_ref[...])
        outs = outs[:-1]
    if out_scale != 1.0:
        z = z * out_scale
    if split:
        outs[0][...] = z

        @pl.when(pl.program_id(0) == pl.num_programs(0) - 1)
        def _():
            outs[1][...] = z[row_tile - N_SAMPLE:, :]
        outs = outs[2:]
    for o in outs:
        o[...] = z.astype(o.dtype)


def _norm_matmul(x, g, w, layer, row_tile, col_tile, act, out_dtypes, out_scale=1.0, split=False, forget=None):
    m = x.shape[0]
    n = w.shape[2]
    grid = (m // row_tile, n // col_tile)
    assert forget is None or col_tile == n
    w_mode = dict(pipeline_mode=pl.Buffered(1)) if col_tile == n else {}
    out_spec = pl.BlockSpec((row_tile, col_tile), lambda i, j: (i, j))
    out_specs = [out_spec] * len(out_dtypes)
    out_shape = [jax.ShapeDtypeStruct((m, n), d) for d in out_dtypes]
    if split:
        last = grid[0] - 1
        sample_spec = pl.BlockSpec((N_SAMPLE, col_tile), lambda i, j: (0, jnp.where(i == last, j, 0)))
        out_specs = [out_spec, sample_spec] + out_specs
        out_shape = [jax.ShapeDtypeStruct((SEQ, n), F32), jax.ShapeDtypeStruct((N_SAMPLE, n), F32)] + out_shape
    in_specs = [
        pl.BlockSpec((row_tile, D_MODEL), lambda i, j: (i, 0)),
        pl.BlockSpec((1, D_MODEL), lambda i, j: (0, 0)),
        pl.BlockSpec((None, D_MODEL, col_tile), lambda i, j: (layer, 0, j), **w_mode),
    ]
    operands = [x, g, w]
    if forget is not None:
        in_specs += [pl.BlockSpec((None, D_MODEL, LANES), lambda i, j: (0, 0, 0)),
                     pl.BlockSpec((1, LANES), lambda i, j: (0, 0))]
        operands += list(forget)
        out_specs = out_specs + [pl.BlockSpec((row_tile, LANES), lambda i, j: (i, 0))]
        out_shape = out_shape + [jax.ShapeDtypeStruct((m, LANES), F32)]
    return pl.pallas_call(
        functools.partial(_norm_matmul_kernel, act=act, n_out=len(out_shape), out_scale=out_scale,
                          split=split, row_tile=row_tile, forget=forget is not None),
        grid=grid,
        in_specs=in_specs,
        out_specs=out_specs,
        out_shape=out_shape,
        scratch_shapes=[pltpu.VMEM((row_tile, D_MODEL), BF16)],
        compiler_params=_params("arbitrary" if split else "parallel", "arbitrary"),
        name="norm_matmul_" + str(act),
    )(*operands)


def _matmul_residual_kernel(a_ref, w_ref, x_ref, o_ref):
    o_ref[...] = x_ref[...] + jnp.dot(a_ref[...], w_ref[...].astype(BF16), preferred_element_type=F32)


def _matmul_residual(a, w, layer, x):
    m, k = a.shape
    n = w.shape[2]
    return pl.pallas_call(
        _matmul_residual_kernel,
        grid=(m // SMALL_ROW_TILE,),
        in_specs=[
            pl.BlockSpec((SMALL_ROW_TILE, k), lambda i: (i, 0)),
            pl.BlockSpec((None, k, n), lambda i: (layer, 0, 0), pipeline_mode=pl.Buffered(1)),
            pl.BlockSpec((SMALL_ROW_TILE, n), lambda i: (i, 0)),
        ],
        out_specs=pl.BlockSpec((SMALL_ROW_TILE, n), lambda i: (i, 0)),
        out_shape=jax.ShapeDtypeStruct((m, n), F32),
        compiler_params=_params("parallel"),
        name="matmul_residual",
    )(a, w, x)


def _gmlp_mix_kernel(u_ref, v_ref, x_ref, lng_ref, lnb_ref, ws_ref, bs_ref, wo_ref,
                     o_ref, vn_ref, gated_ref, *, n_prompt_tiles):
    i = pl.program_id(0)
    v = v_ref[...]
    mu = jnp.mean(v, axis=-1, keepdims=True)
    vc = v - mu
    vn = vc * lax.rsqrt(jnp.mean(vc * vc, axis=-1, keepdims=True) + LN_EPS)
    vn = vn * lng_ref[...] + lnb_ref[...]

    is_sample = i >= n_prompt_tiles

    @pl.when(is_sample)
    def _():
        vn_ref[...] = vn

    sel = is_sample.astype(jnp.int32)
    shift = jnp.where(is_sample, int(math.log2(DEC_SEQ)), int(math.log2(GMLP_CHUNK)))
    t = lax.broadcasted_iota(jnp.int32, (GMLP_CHUNK, GMLP_CHUNK), 0)
    s = lax.broadcasted_iota(jnp.int32, (GMLP_CHUNK, GMLP_CHUNK), 1)
    mask = (s <= t) & ((t >> shift) == (s >> shift))
    vnb = vn.astype(BF16)
    for grp in range(GMLP_GROUPS):
        w = jnp.where(mask, ws_ref[sel, grp], 0.0).astype(BF16)
        bias = bs_ref[sel, grp]
        cols = slice(grp * GROUP_W, (grp + 1) * GROUP_W)
        for c in range(GMLP_ROW_TILE // GMLP_CHUNK):
            rows = slice(c * GMLP_CHUNK, (c + 1) * GMLP_CHUNK)
            mixed = jnp.dot(w, vnb[rows, cols], preferred_element_type=F32) + bias
            gated_ref[rows, cols] = (u_ref[rows, cols] * mixed).astype(BF16)
    o_ref[...] = x_ref[...] + jnp.dot(gated_ref[...], wo_ref[...], preferred_element_type=F32)


def _gmlp_mix(z, x, ln_g, ln_b, ws2, bs2, w_out):
    m = x.shape[0]
    n_prompt_tiles = SEQ // GMLP_ROW_TILE
    single = pl.Buffered(1)
    return pl.pallas_call(
        functools.partial(_gmlp_mix_kernel, n_prompt_tiles=n_prompt_tiles),
        grid=(m // GMLP_ROW_TILE,),
        in_specs=[
            pl.BlockSpec((GMLP_ROW_TILE, D_GATE), lambda i: (i, 0)),
            pl.BlockSpec((GMLP_ROW_TILE, D_GATE), lambda i: (i, 1)),
            pl.BlockSpec((GMLP_ROW_TILE, D_MODEL), lambda i: (i, 0)),
            pl.BlockSpec((1, D_GATE), lambda i: (0, 0)),
            pl.BlockSpec((1, D_GATE), lambda i: (0, 0)),
            pl.BlockSpec((2, GMLP_GROUPS, GMLP_CHUNK, GMLP_CHUNK), lambda i: (0, 0, 0, 0)),
            pl.BlockSpec((2, GMLP_GROUPS, GMLP_CHUNK, 1), lambda i: (0, 0, 0, 0)),
            pl.BlockSpec((D_GATE, D_MODEL), lambda i: (0, 0), pipeline_mode=single),
        ],
        out_specs=[
            pl.BlockSpec((GMLP_ROW_TILE, D_MODEL), lambda i: (i, 0)),
            pl.BlockSpec((GMLP_ROW_TILE, D_GATE),
                         lambda i: (jnp.maximum(i - n_prompt_tiles, 0), 0), pipeline_mode=single),
        ],
        out_shape=[
            jax.ShapeDtypeStruct((m, D_MODEL), F32),
            jax.ShapeDtypeStruct((N_SAMPLE, D_GATE), F32),
        ],
        scratch_shapes=[pltpu.VMEM((GMLP_ROW_TILE, D_GATE), BF16)],
        compiler_params=_params("arbitrary"),
        name="gmlp_mix",
    )(z, z, x, ln_g, ln_b, ws2, bs2, w_out)


def _scan_kernel(x_ref, o_ref):
    x = x_ref[...]
    n = x.shape[-1]
    idx = lax.broadcasted_iota(jnp.int32, x.shape, 1)
    step = 1
    while step < n:
        x = x + jnp.where(idx >= step, pltpu.roll(x, step, axis=1), 0.0)
        step *= 2
    o_ref[...] = x * LOG2E


def _cumsum_lanes(x):
    return pl.pallas_call(
        _scan_kernel,
        out_shape=jax.ShapeDtypeStruct(x.shape, F32),
        compiler_params=pltpu.CompilerParams(vmem_limit_bytes=VMEM_LIMIT),
        name="cumsum_lanes",
    )(x)


def _tile_lanes(x, reps):
    return jnp.concatenate([x] * reps, axis=1)


def _softmax_step(logits, v, m_ref, acc_ref):
    reps = logits.shape[1] // LANES
    m_prev = m_ref[...]
    m_new = jnp.maximum(m_prev, jnp.max(logits, axis=1, keepdims=True))
    p = jnp.exp2(logits - _tile_lanes(m_new, reps))
    alpha = jnp.exp2(m_prev - m_new)
    v_ones = jnp.concatenate([v, jnp.ones_like(v)], axis=1)
    pv = jnp.dot(p.astype(BF16), v_ones, preferred_element_type=F32)
    acc_ref[...] = _tile_lanes(alpha, 2) * acc_ref[...] + pv
    m_ref[...] = m_new


def _softmax_result(acc_ref):
    acc = acc_ref[...]
    return acc[:, :HEAD_DIM] / acc[:, HEAD_DIM:]


def _fox_prompt_kernel(q_ref, k_ref, v_ref, c_ref, cq_rows_ref, o_ref, m_ref, acc_ref, cq_ref):
    qi = pl.program_id(1)
    per_key_tile = ATT_TILE // ATT_Q_TILE
    diag = qi // per_key_tile
    first_row = (qi % per_key_tile) * ATT_Q_TILE
    for hh in range(HEADS_PER_STEP):
        cq_ref[hh] = jnp.broadcast_to(cq_rows_ref[hh, qi], (LANES, ATT_Q_TILE)).T
    m_ref[...] = jnp.full_like(m_ref, NEG_INF)
    acc_ref[...] = jnp.zeros_like(acc_ref)

    def block(kj, on_diagonal):
        start = pl.multiple_of(kj * ATT_TILE, ATT_TILE)
        for hh in range(HEADS_PER_STEP):
            cols = slice(hh * HEAD_DIM, (hh + 1) * HEAD_DIM)
            k = k_ref[pl.ds(start, ATT_TILE), cols]
            s = lax.dot_general(q_ref[:, cols], k, (((1,), (1,)), ((), ())), preferred_element_type=F32)
            logits = s + (_tile_lanes(cq_ref[hh], ATT_TILE // LANES) - c_ref[hh, kj])
            if on_diagonal:
                row = lax.broadcasted_iota(jnp.int32, (ATT_Q_TILE, ATT_TILE), 0) + first_row
                col = lax.broadcasted_iota(jnp.int32, (ATT_Q_TILE, ATT_TILE), 1)
                logits = jnp.where(row >= col, logits, NEG_INF)
            _softmax_step(logits, v_ref[pl.ds(start, ATT_TILE), cols], m_ref.at[hh], acc_ref.at[hh])

    def body(kj, carry):
        block(kj, False)
        return carry

    lax.fori_loop(0, diag, body, 0)
    block(diag, True)
    for hh in range(HEADS_PER_STEP):
        cols = slice(hh * HEAD_DIM, (hh + 1) * HEAD_DIM)
        o_ref[:, cols] = _softmax_result(acc_ref.at[hh]).astype(o_ref.dtype)


def _fox_prompt(q, k, v, c_tiles):
    n_tiles = SEQ // ATT_TILE
    n_q_tiles = SEQ // ATT_Q_TILE
    width = HEADS_PER_STEP * HEAD_DIM
    cq_rows = c_tiles.reshape(N_HEADS, n_q_tiles, 1, ATT_Q_TILE)
    return pl.pallas_call(
        _fox_prompt_kernel,
        grid=(N_HEADS // HEADS_PER_STEP, n_q_tiles),
        in_specs=[
            pl.BlockSpec((ATT_Q_TILE, width), lambda h, i: (i, h)),
            pl.BlockSpec((SEQ, width), lambda h, i: (0, h), pipeline_mode=pl.Buffered(1)),
            pl.BlockSpec((SEQ, width), lambda h, i: (0, h), pipeline_mode=pl.Buffered(1)),
            pl.BlockSpec((HEADS_PER_STEP, n_tiles, 1, ATT_TILE), lambda h, i: (h, 0, 0, 0)),
            pl.BlockSpec((HEADS_PER_STEP, n_q_tiles, 1, ATT_Q_TILE), lambda h, i: (h, 0, 0, 0)),
        ],
        out_specs=pl.BlockSpec((ATT_Q_TILE, width), lambda h, i: (i, h)),
        out_shape=jax.ShapeDtypeStruct((SEQ, D_MODEL), BF16),
        scratch_shapes=[
            pltpu.VMEM((HEADS_PER_STEP, ATT_Q_TILE, LANES), F32),
            pltpu.VMEM((HEADS_PER_STEP, ATT_Q_TILE, 2 * HEAD_DIM), F32),
            pltpu.VMEM((HEADS_PER_STEP, ATT_Q_TILE, LANES), F32),
        ],
        compiler_params=_params("parallel", "arbitrary"),
        name="fox_prompt",
    )(q, k, v, c_tiles, cq_rows)


def _fox_sample_kernel(q_ref, kc_ref, vc_ref, kn_ref, vn_ref, cc_ref, cn_ref, o_ref, m_ref, acc_ref):
    j = pl.program_id(1)

    @pl.when(j == 0)
    def _():
        m_ref[...] = jnp.full_like(m_ref, NEG_INF)
        acc_ref[...] = jnp.zeros_like(acc_ref)

    c_new = cn_ref[...]
    c_pad = jnp.concatenate([c_new, jnp.zeros((LANES - N_HEADS, LANES), F32)], axis=0)
    cq_all = c_pad.T[:DEC_SEQ, :]

    def cq_of(h):
        return jnp.broadcast_to(cq_all[:, h:h + 1], (DEC_SEQ, LANES))

    for h in range(N_HEADS):
        cols = slice(h * HEAD_DIM, (h + 1) * HEAD_DIM)
        head_rows = pl.ds(h, CACHE_TILE, stride=N_HEADS)
        k = kc_ref[head_rows, :].astype(BF16)
        s = lax.dot_general(q_ref[:, cols], k, (((1,), (1,)), ((), ())), preferred_element_type=F32)
        logits = s + (_tile_lanes(cq_of(h), CACHE_TILE // LANES) - cc_ref[h:h + 1, :])
        _softmax_step(logits, vc_ref[head_rows, :].astype(BF16), m_ref.at[h], acc_ref.at[h])

    @pl.when(j == pl.num_programs(1) - 1)
    def _():
        row = lax.broadcasted_iota(jnp.int32, (DEC_SEQ, LANES), 0)
        col = lax.broadcasted_iota(jnp.int32, (DEC_SEQ, LANES), 1)
        pad = jnp.zeros((LANES - DEC_SEQ, HEAD_DIM), BF16)
        for h in range(N_HEADS):
            cols = slice(h * HEAD_DIM, (h + 1) * HEAD_DIM)
            k = jnp.concatenate([kn_ref[:, cols], pad], axis=0)
            v = jnp.concatenate([vn_ref[:, cols], pad], axis=0)
            s = lax.dot_general(q_ref[:, cols], k, (((1,), (1,)), ((), ())), preferred_element_type=F32)
            logits = s + (cq_of(h) - c_new[h:h + 1, :])
            logits = jnp.where(row >= col, logits, NEG_INF)
            _softmax_step(logits, v, m_ref.at[h], acc_ref.at[h])
            o_ref[:, cols] = _softmax_result(acc_ref.at[h]).astype(o_ref.dtype)


def _fox_sample(q, k_new, v_new, cache_k, cache_v, c_sample):
    first = SEQ // DEC_SEQ
    n_steps = PAST_LEN // CACHE_TILE
    new_spec = pl.BlockSpec((DEC_SEQ, D_MODEL), lambda b, j: (first + b, 0))
    cache_spec = pl.BlockSpec((None, CACHE_TILE * N_HEADS, HEAD_DIM), lambda b, j: (b, j, 0))
    cache_k = cache_k.reshape(DEC_BATCH, PAST_LEN * N_HEADS, HEAD_DIM)
    cache_v = cache_v.reshape(DEC_BATCH, PAST_LEN * N_HEADS, HEAD_DIM)
    return pl.pallas_call(
        _fox_sample_kernel,
        grid=(DEC_BATCH, n_steps),
        in_specs=[
            new_spec, cache_spec, cache_spec, new_spec, new_spec,
            pl.BlockSpec((None, N_HEADS, CACHE_TILE), lambda b, j: (b, 0, j)),
            pl.BlockSpec((None, N_HEADS, LANES), lambda b, j: (b, 0, PAST_LEN // LANES)),
        ],
        out_specs=pl.BlockSpec((DEC_SEQ, D_MODEL), lambda b, j: (b, 0)),
        out_shape=jax.ShapeDtypeStruct((N_SAMPLE, D_MODEL), BF16),
        scratch_shapes=[
            pltpu.VMEM((N_HEADS, DEC_SEQ, LANES), F32),
            pltpu.VMEM((N_HEADS, DEC_SEQ, 2 * HEAD_DIM), F32),
        ],
        compiler_params=_params("parallel", "arbitrary"),
        name="fox_sample",
    )(q, cache_k, cache_v, k_new, v_new, c_sample, c_sample)


def kernel(x_prompt, x_sample, cache_k, cache_v, cache_logf, ffn1_norm, ffn1_w_gate, ffn1_w_up,
           ffn1_w_down, mix_norm, ffn2_norm, ffn2_w_gate, ffn2_w_up, ffn2_w_down, gmlp_w_in,
           gmlp_ln_g, gmlp_ln_b, gmlp_w_s, gmlp_b_s, gmlp_w_out, kv_norm, w_k, w_v, w_f, b_f,
           fox_w_q, fox_w_o, final_norm):
    bf = lambda w: w.astype(BF16)
    x = lax.dynamic_update_slice(jnp.pad(x_prompt.reshape(SEQ, D_MODEL), ((0, N_SAMPLE), (0, 0))),
                                 x_sample.reshape(N_SAMPLE, D_MODEL), (SEQ, 0))

    ffn1 = (ffn1_norm, ffn1_w_gate, ffn1_w_up, ffn1_w_down)
    ffn2 = (ffn2_norm, ffn2_w_gate, ffn2_w_up, ffn2_w_down)
    final_g = final_norm.reshape(1, D_MODEL)
    w_in_bf = bf(gmlp_w_in)

    reps = GMLP_CHUNK // DEC_SEQ
    ws2 = jnp.stack([gmlp_w_s, jnp.tile(gmlp_w_s[:, :, :DEC_SEQ, :DEC_SEQ], (1, 1, reps, reps))], axis=1)
    bs2 = jnp.stack([gmlp_b_s, jnp.tile(gmlp_b_s[:, :, :DEC_SEQ], (1, 1, reps))], axis=1)[..., None]

    gmlp_v = []
    logf_all = None
    for l in range(DEPTH):
        if l == N_A_LAYERS:
            kvg = kv_norm.reshape(1, D_MODEL)
            w_f_pad = jnp.pad(w_f, ((0, 0), (0, LANES - N_HEADS)))[None]
            b_f_pad = jnp.pad(b_f, (0, LANES - N_HEADS)).reshape(1, LANES)
            k_prompt, k_sample, k_bf = _norm_matmul(x, kvg, w_k[None], 0, SMALL_ROW_TILE, D_MODEL, None,
                                                    (BF16,), split=True)
            v_prompt, v_sample, v_bf, logf_pad = _norm_matmul(x, kvg, w_v[None], 0, SMALL_ROW_TILE, D_MODEL, None,
                                                              (BF16,), split=True, forget=(w_f_pad, b_f_pad))
            logf_all = logf_pad[:, :N_HEADS]
            c_prompt = _cumsum_lanes(logf_all[:SEQ].T)
            c_prompt = c_prompt.reshape(N_HEADS, SEQ // ATT_TILE, 1, ATT_TILE)
            lf_new = logf_all[SEQ:].reshape(DEC_BATCH, DEC_SEQ, N_HEADS).transpose(0, 2, 1)
            lf_cache = cache_logf.transpose(0, 2, 1)
            lf = jnp.concatenate(
                [lf_cache, lf_new, jnp.zeros((DEC_BATCH, N_HEADS, LANES - DEC_SEQ), F32)], axis=-1)
            c_sample = _cumsum_lanes(lf.reshape(DEC_BATCH * N_HEADS, PAST_LEN + LANES))
            c_sample = c_sample.reshape(DEC_BATCH, N_HEADS, PAST_LEN + LANES)

        x = _half_ffn(x, *ffn1, l, final_g, False)
        mg = mix_norm[l].reshape(1, D_MODEL)
        if l < N_A_LAYERS:
            (z,) = _norm_matmul(x, mg, w_in_bf, l, ROW_TILE, 1024, "gelu", (F32,))
            x, vn = _gmlp_mix(z, x, gmlp_ln_g[l].reshape(1, D_GATE), gmlp_ln_b[l].reshape(1, D_GATE),
                              ws2[l], bs2[l], bf(gmlp_w_out[l]))
            gmlp_v.append(vn.reshape(DEC_BATCH, DEC_SEQ, D_GATE))
        else:
            jj = l - N_A_LAYERS
            (q,) = _norm_matmul(x, mg, fox_w_q, jj, SMALL_ROW_TILE, D_MODEL, None, (BF16,),
                                out_scale=HEAD_DIM ** -0.5 * LOG2E)
            o_prompt = _fox_prompt(q, k_bf, v_bf, c_prompt)
            o_sample = _fox_sample(q, k_bf, v_bf, cache_k, cache_v, c_sample)
            o = jnp.concatenate([o_prompt, o_sample], axis=0)
            x = _matmul_residual(o, fox_w_o, jj, x)
        x = _half_ffn(x, *ffn2, l, final_g, l == DEPTH - 1)

    y_prompt, y_sample = x
    hd = (N_HEADS, HEAD_DIM)
    return (y_prompt.reshape(1, SEQ, D_MODEL),
            y_sample.reshape(DEC_BATCH, DEC_SEQ, D_MODEL),
            k_prompt.reshape(1, SEQ, *hd),
            v_prompt.reshape(1, SEQ, *hd),
            logf_all[:SEQ].reshape(1, SEQ, N_HEADS),
            k_sample.reshape(DEC_BATCH, DEC_SEQ, *hd),
            v_sample.reshape(DEC_BATCH, DEC_SEQ, *hd),
            logf_all[SEQ:].reshape(DEC_BATCH, DEC_SEQ, N_HEADS),
            jnp.stack(gmlp_v, axis=0))
```

```python
import functools
import math

import jax
import jax.numpy as jnp
from jax import lax
from jax.experimental import pallas as pl
from jax.experimental.pallas import tpu as pltpu

D_MODEL = 2048
SEQ = 8192
DEPTH = 4
DEC_BATCH = 16
DEC_SEQ = 16
PAST_LEN = 2048
N_A_LAYERS = DEPTH // 2
D_FF = 5632
GMLP_CHUNK = 128
D_GATE = 2 * D_MODEL
GMLP_GROUPS = 4
GROUP_W = D_GATE // GMLP_GROUPS
N_HEADS = 16
HEAD_DIM = D_MODEL // N_HEADS
RMS_EPS = 1e-6
LN_EPS = 1e-5
NEG_INF = -1e30
LOG2E = math.log2(math.e)

N_SAMPLE = DEC_BATCH * DEC_SEQ
M_ROWS = SEQ + N_SAMPLE
LANES = 128

F32 = jnp.float32
BF16 = jnp.bfloat16

ROW_TILE = 768
SMALL_ROW_TILE = 384
FFN_ROW_TILE = 1056
FF_TILE = 512
FFN_CAST_TILE = 128
GMLP_ROW_TILE = 256
ATT_TILE = 512
ATT_Q_TILE = 512
HEADS_PER_STEP = 8
CACHE_TILE = 512
VMEM_LIMIT = 58 * 1024 * 1024


def _params(*sem):
    return pltpu.CompilerParams(dimension_semantics=sem, vmem_limit_bytes=VMEM_LIMIT)


def _rms(xf, g):
    y = xf * lax.rsqrt(jnp.mean(xf * xf, axis=-1, keepdims=True) + RMS_EPS)
    return y * g


def _ffn_kernel(*refs, last):
    x_ref, g_ref, wg_ref, wu_ref, wd_ref = refs[:5]
    h_ref = refs[-1]
    i, j = pl.program_id(0), pl.program_id(1)
    if last:
        gf_ref, o_ref, sample_ref = refs[5:8]
    else:
        next_refs, o_ref, cast_refs = refs[5:8], refs[8], refs[9:12]

        @pl.when(i * pl.num_programs(1) + j < D_FF // FFN_CAST_TILE)
        def _():
            for src, dst in zip(next_refs, cast_refs):
                dst[...] = src[...].astype(BF16)

    @pl.when(j == 0)
    def _():
        h_ref[...] = _rms(x_ref[...], g_ref[...]).astype(BF16)
        o_ref[...] = jnp.zeros_like(o_ref)

    h = h_ref[...]
    a = jnp.dot(h, wg_ref[...], preferred_element_type=F32)
    b = jnp.dot(h, wu_ref[...], preferred_element_type=F32)
    act = (a * jax.nn.sigmoid(a) * b).astype(BF16)
    o_ref[...] += jnp.dot(act, wd_ref[...], preferred_element_type=F32)

    @pl.when(j == pl.num_programs(1) - 1)
    def _():
        r = x_ref[...] + 0.5 * o_ref[...]
        if last:
            r = _rms(r, gf_ref[...])
        o_ref[...] = r
        if last:
            @pl.when(i == pl.num_programs(0) - 1)
            def _():
                sample_ref[...] = r[FFN_ROW_TILE - N_SAMPLE:, :]


def _half_ffn(x, g, weights, next_weights=None, next_layer=None, final_gain=None):
    m = x.shape[0]
    n_ff = D_FF // FF_TILE
    grid = (m // FFN_ROW_TILE, n_ff)
    last = next_weights is None
    row_spec = pl.BlockSpec((FFN_ROW_TILE, D_MODEL), lambda i, j: (i, 0), pipeline_mode=pl.Buffered(1))
    in_specs = [
        pl.BlockSpec((FFN_ROW_TILE, D_MODEL), lambda i, j: (i, 0)),
        pl.BlockSpec((1, D_MODEL), lambda i, j: (0, 0)),
        pl.BlockSpec((D_MODEL, FF_TILE), lambda i, j: (0, j)),
        pl.BlockSpec((D_MODEL, FF_TILE), lambda i, j: (0, j)),
        pl.BlockSpec((FF_TILE, D_MODEL), lambda i, j: (j, 0)),
    ]
    if last:
        in_specs.append(pl.BlockSpec((1, D_MODEL), lambda i, j: (0, 0)))
        operands = (x, g, *weights, final_gain)
        out_specs = [row_spec, pl.BlockSpec((N_SAMPLE, D_MODEL), lambda i, j: (0, 0))]
        out_shape = [jax.ShapeDtypeStruct((SEQ, D_MODEL), F32), jax.ShapeDtypeStruct((N_SAMPLE, D_MODEL), F32)]
    else:
        slab = lambda i, j: jnp.minimum(i * n_ff + j, D_FF // FFN_CAST_TILE - 1)
        in_specs += [
            pl.BlockSpec((None, D_MODEL, FFN_CAST_TILE), lambda i, j: (next_layer, 0, slab(i, j))),
            pl.BlockSpec((None, D_MODEL, FFN_CAST_TILE), lambda i, j: (next_layer, 0, slab(i, j))),
            pl.BlockSpec((None, FFN_CAST_TILE, D_MODEL), lambda i, j: (next_layer, slab(i, j), 0)),
        ]
        operands = (x, g, *weights, *next_weights)
        out_specs = [
            row_spec,
            pl.BlockSpec((D_MODEL, FFN_CAST_TILE), lambda i, j: (0, slab(i, j))),
            pl.BlockSpec((D_MODEL, FFN_CAST_TILE), lambda i, j: (0, slab(i, j))),
            pl.BlockSpec((FFN_CAST_TILE, D_MODEL), lambda i, j: (slab(i, j), 0)),
        ]
        out_shape = [
            jax.ShapeDtypeStruct((m, D_MODEL), F32),
            jax.ShapeDtypeStruct((D_MODEL, D_FF), BF16),
            jax.ShapeDtypeStruct((D_MODEL, D_FF), BF16),
            jax.ShapeDtypeStruct((D_FF, D_MODEL), BF16),
        ]
    return pl.pallas_call(
        functools.partial(_ffn_kernel, last=last),
        grid=grid,
        in_specs=in_specs,
        out_specs=out_specs,
        out_shape=out_shape,
        scratch_shapes=[pltpu.VMEM((FFN_ROW_TILE, D_MODEL), BF16)],
        compiler_params=_params("arbitrary", "arbitrary"),
        name="half_ffn",
    )(*operands)


def _gelu(z):
    return 0.5 * z * (1.0 + lax.erf(z * (1.0 / math.sqrt(2.0))))


def _log_sigmoid(z):
    return -(jnp.maximum(-z, 0.0) + jnp.log1p(jnp.exp(-jnp.abs(z))))


def _norm_matmul_kernel(x_ref, g_ref, w_ref, *rest, act, n_out, out_scale, split, row_tile, forget):
    if forget:
        wf_ref, bf_ref, rest = rest[0], rest[1], rest[2:]
    outs, h_ref = rest[:n_out], rest[n_out]
    j = pl.program_id(1)

    @pl.when(j == 0)
    def _():
        h_ref[...] = _rms(x_ref[...], g_ref[...]).astype(BF16)

    z = jnp.dot(h_ref[...], w_ref[...].astype(BF16), preferred_element_type=F32)
    if act == "gelu":
        z = _gelu(z)
    if forget:
        zf = jnp.dot(h_ref[...], wf_ref[...].astype(BF16), preferred_element_type=F32)
        outs[-1][...] = _log_sigmoid(zf + bf_ref[...])
        outs = outs[:-1]
    if out_scale != 1.0:
        z = z * out_scale
    if split:
        outs[0][...] = z

        @pl.when(pl.program_id(0) == pl.num_programs(0) - 1)
        def _():
            outs[1][...] = z[row_tile - N_SAMPLE:, :]
        outs = outs[2:]
    for o in outs:
        o[...] = z.astype(o.dtype)


def _norm_matmul(x, g, w, layer, row_tile, col_tile, act, out_dtypes, out_scale=1.0, split=False, forget=None):
    m = x.shape[0]
    n = w.shape[2]
    grid = (m // row_tile, n // col_tile)
    assert forget is None or col_tile == n
    w_mode = dict(pipeline_mode=pl.Buffered(1)) if col_tile == n else {}
    out_spec = pl.BlockSpec((row_tile, col_tile), lambda i, j: (i, j))
    out_specs = [out_spec] * len(out_dtypes)
    out_shape = [jax.ShapeDtypeStruct((m, n), d) for d in out_dtypes]
    if split:
        last = grid[0] - 1
        sample_spec = pl.BlockSpec((N_SAMPLE, col_tile), lambda i, j: (0, jnp.where(i == last, j, 0)))
        out_specs = [out_spec, sample_spec] + out_specs
        out_shape = [jax.ShapeDtypeStruct((SEQ, n), F32), jax.ShapeDtypeStruct((N_SAMPLE, n), F32)] + out_shape
    in_specs = [
        pl.BlockSpec((row_tile, D_MODEL), lambda i, j: (i, 0)),
        pl.BlockSpec((1, D_MODEL), lambda i, j: (0, 0)),
        pl.BlockSpec((None, D_MODEL, col_tile), lambda i, j: (layer, 0, j), **w_mode),
    ]
    operands = [x, g, w]
    if forget is not None:
        in_specs += [pl.BlockSpec((None, D_MODEL, LANES), lambda i, j: (0, 0, 0)),
                     pl.BlockSpec((1, LANES), lambda i, j: (0, 0))]
        operands += list(forget)
        out_specs = out_specs + [pl.BlockSpec((row_tile, LANES), lambda i, j: (i, 0))]
        out_shape = out_shape + [jax.ShapeDtypeStruct((m, LANES), F32)]
    return pl.pallas_call(
        functools.partial(_norm_matmul_kernel, act=act, n_out=len(out_shape), out_scale=out_scale,
                          split=split, row_tile=row_tile, forget=forget is not None),
        grid=grid,
        in_specs=in_specs,
        out_specs=out_specs,
        out_shape=out_shape,
        scratch_shapes=[pltpu.VMEM((row_tile, D_MODEL), BF16)],
        compiler_params=_params("arbitrary" if split else "parallel", "arbitrary"),
        name="norm_matmul_" + str(act),
    )(*operands)


def _matmul_residual_kernel(a_ref, w_ref, x_ref, o_ref):
    o_ref[...] = x_ref[...] + jnp.dot(a_ref[...], w_ref[...].astype(BF16), preferred_element_type=F32)


def _matmul_residual(a, w, layer, x):
    m, k = a.shape
    n = w.shape[2]
    return pl.pallas_call(
        _matmul_residual_kernel,
        grid=(m // SMALL_ROW_TILE,),
        in_specs=[
            pl.BlockSpec((SMALL_ROW_TILE, k), lambda i: (i, 0)),
            pl.BlockSpec((None, k, n), lambda i: (layer, 0, 0), pipeline_mode=pl.Buffered(1)),
            pl.BlockSpec((SMALL_ROW_TILE, n), lambda i: (i, 0)),
        ],
        out_specs=pl.BlockSpec((SMALL_ROW_TILE, n), lambda i: (i, 0)),
        out_shape=jax.ShapeDtypeStruct((m, n), F32),
        compiler_params=_params("parallel"),
        name="matmul_residual",
    )(a, w, x)


def _gmlp_mix_kernel(u_ref, v_ref, x_ref, lng_ref, lnb_ref, ws_ref, bs_ref, wo_ref,
                     o_ref, vn_ref, gated_ref, *, n_prompt_tiles):
    i = pl.program_id(0)
    v = v_ref[...]
    mu = jnp.mean(v, axis=-1, keepdims=True)
    vc = v - mu
    vn = vc * lax.rsqrt(jnp.mean(vc * vc, axis=-1, keepdims=True) + LN_EPS)
    vn = vn * lng_ref[...] + lnb_ref[...]

    is_sample = i >= n_prompt_tiles

    @pl.when(is_sample)
    def _():
        vn_ref[...] = vn

    sel = is_sample.astype(jnp.int32)
    shift = jnp.where(is_sample, int(math.log2(DEC_SEQ)), int(math.log2(GMLP_CHUNK)))
    t = lax.broadcasted_iota(jnp.int32, (GMLP_CHUNK, GMLP_CHUNK), 0)
    s = lax.broadcasted_iota(jnp.int32, (GMLP_CHUNK, GMLP_CHUNK), 1)
    mask = (s <= t) & ((t >> shift) == (s >> shift))
    vnb = vn.astype(BF16)
    for grp in range(GMLP_GROUPS):
        w = jnp.where(mask, ws_ref[sel, grp], 0.0).astype(BF16)
        bias = bs_ref[sel, grp]
        cols = slice(grp * GROUP_W, (grp + 1) * GROUP_W)
        for c in range(GMLP_ROW_TILE // GMLP_CHUNK):
            rows = slice(c * GMLP_CHUNK, (c + 1) * GMLP_CHUNK)
            mixed = jnp.dot(w, vnb[rows, cols], preferred_element_type=F32) + bias
            gated_ref[rows, cols] = (u_ref[rows, cols] * mixed).astype(BF16)
    o_ref[...] = x_ref[...] + jnp.dot(gated_ref[...], wo_ref[...], preferred_element_type=F32)


def _gmlp_mix(z, x, ln_g, ln_b, ws2, bs2, w_out):
    m = x.shape[0]
    n_prompt_tiles = SEQ // GMLP_ROW_TILE
    single = pl.Buffered(1)
    return pl.pallas_call(
        functools.partial(_gmlp_mix_kernel, n_prompt_tiles=n_prompt_tiles),
        grid=(m // GMLP_ROW_TILE,),
        in_specs=[
            pl.BlockSpec((GMLP_ROW_TILE, D_GATE), lambda i: (i, 0)),
            pl.BlockSpec((GMLP_ROW_TILE, D_GATE), lambda i: (i, 1)),
            pl.BlockSpec((GMLP_ROW_TILE, D_MODEL), lambda i: (i, 0)),
            pl.BlockSpec((1, D_GATE), lambda i: (0, 0)),
            pl.BlockSpec((1, D_GATE), lambda i: (0, 0)),
            pl.BlockSpec((2, GMLP_GROUPS, GMLP_CHUNK, GMLP_CHUNK), lambda i: (0, 0, 0, 0)),
            pl.BlockSpec((2, GMLP_GROUPS, GMLP_CHUNK, 1), lambda i: (0, 0, 0, 0)),
            pl.BlockSpec((D_GATE, D_MODEL), lambda i: (0, 0), pipeline_mode=single),
        ],
        out_specs=[
            pl.BlockSpec((GMLP_ROW_TILE, D_MODEL), lambda i: (i, 0)),
            pl.BlockSpec((GMLP_ROW_TILE, D_GATE),
                         lambda i: (jnp.maximum(i - n_prompt_tiles, 0), 0), pipeline_mode=single),
        ],
        out_shape=[
            jax.ShapeDtypeStruct((m, D_MODEL), F32),
            jax.ShapeDtypeStruct((N_SAMPLE, D_GATE), F32),
        ],
        scratch_shapes=[pltpu.VMEM((GMLP_ROW_TILE, D_GATE), BF16)],
        compiler_params=_params("arbitrary"),
        name="gmlp_mix",
    )(z, z, x, ln_g, ln_b, ws2, bs2, w_out)


def _scan_kernel(x_ref, o_ref):
    x = x_ref[...]
    n = x.shape[-1]
    idx = lax.broadcasted_iota(jnp.int32, x.shape, 1)
    step = 1
    while step < n:
        x = x + jnp.where(idx >= step, pltpu.roll(x, step, axis=1), 0.0)
        step *= 2
    o_ref[...] = x * LOG2E


def _cumsum_lanes(x):
    return pl.pallas_call(
        _scan_kernel,
        out_shape=jax.ShapeDtypeStruct(x.shape, F32),
        compiler_params=pltpu.CompilerParams(vmem_limit_bytes=VMEM_LIMIT),
        name="cumsum_lanes",
    )(x)


def _tile_lanes(x, reps):
    return jnp.concatenate([x] * reps, axis=1)


def _softmax_step(logits, v, m_ref, acc_ref):
    reps = logits.shape[1] // LANES
    m_prev = m_ref[...]
    m_new = jnp.maximum(m_prev, jnp.max(logits, axis=1, keepdims=True))
    p = jnp.exp2(logits - _tile_lanes(m_new, reps))
    alpha = jnp.exp2(m_prev - m_new)
    v_ones = jnp.concatenate([v, jnp.ones_like(v)], axis=1)
    pv = jnp.dot(p.astype(BF16), v_ones, preferred_element_type=F32)
    acc_ref[...] = _tile_lanes(alpha, 2) * acc_ref[...] + pv
    m_ref[...] = m_new


def _softmax_result(acc_ref):
    acc = acc_ref[...]
    return acc[:, :HEAD_DIM] / acc[:, HEAD_DIM:]


def _fox_prompt_kernel(q_ref, k_ref, v_ref, c_ref, cq_rows_ref, o_ref, m_ref, acc_ref, cq_ref):
    qi = pl.program_id(1)
    per_key_tile = ATT_TILE // ATT_Q_TILE
    diag = qi // per_key_tile
    first_row = (qi % per_key_tile) * ATT_Q_TILE
    for hh in range(HEADS_PER_STEP):
        cq_ref[hh] = jnp.broadcast_to(cq_rows_ref[hh, qi], (LANES, ATT_Q_TILE)).T
    m_ref[...] = jnp.full_like(m_ref, NEG_INF)
    acc_ref[...] = jnp.zeros_like(acc_ref)

    def block(kj, on_diagonal):
        start = pl.multiple_of(kj * ATT_TILE, ATT_TILE)
        for hh in range(HEADS_PER_STEP):
            cols = slice(hh * HEAD_DIM, (hh + 1) * HEAD_DIM)
            k = k_ref[pl.ds(start, ATT_TILE), cols]
            s = lax.dot_general(q_ref[:, cols], k, (((1,), (1,)), ((), ())), preferred_element_type=F32)
            logits = s + (_tile_lanes(cq_ref[hh], ATT_TILE // LANES) - c_ref[hh, kj])
            if on_diagonal:
                row = lax.broadcasted_iota(jnp.int32, (ATT_Q_TILE, ATT_TILE), 0) + first_row
                col = lax.broadcasted_iota(jnp.int32, (ATT_Q_TILE, ATT_TILE), 1)
                logits = jnp.where(row >= col, logits, NEG_INF)
            _softmax_step(logits, v_ref[pl.ds(start, ATT_TILE), cols], m_ref.at[hh], acc_ref.at[hh])

    def body(kj, carry):
        block(kj, False)
        return carry

    lax.fori_loop(0, diag, body, 0)
    block(diag, True)
    for hh in range(HEADS_PER_STEP):
        cols = slice(hh * HEAD_DIM, (hh + 1) * HEAD_DIM)
        o_ref[:, cols] = _softmax_result(acc_ref.at[hh]).astype(o_ref.dtype)


def _fox_prompt(q, k, v, c_tiles):
    n_tiles = SEQ // ATT_TILE
    n_q_tiles = SEQ // ATT_Q_TILE
    width = HEADS_PER_STEP * HEAD_DIM
    cq_rows = c_tiles.reshape(N_HEADS, n_q_tiles, 1, ATT_Q_TILE)
    return pl.pallas_call(
        _fox_prompt_kernel,
        grid=(N_HEADS // HEADS_PER_STEP, n_q_tiles),
        in_specs=[
            pl.BlockSpec((ATT_Q_TILE, width), lambda h, i: (i, h)),
            pl.BlockSpec((SEQ, width), lambda h, i: (0, h), pipeline_mode=pl.Buffered(1)),
            pl.BlockSpec((SEQ, width), lambda h, i: (0, h), pipeline_mode=pl.Buffered(1)),
            pl.BlockSpec((HEADS_PER_STEP, n_tiles, 1, ATT_TILE), lambda h, i: (h, 0, 0, 0)),
            pl.BlockSpec((HEADS_PER_STEP, n_q_tiles, 1, ATT_Q_TILE), lambda h, i: (h, 0, 0, 0)),
        ],
        out_specs=pl.BlockSpec((ATT_Q_TILE, width), lambda h, i: (i, h)),
        out_shape=jax.ShapeDtypeStruct((SEQ, D_MODEL), BF16),
        scratch_shapes=[
            pltpu.VMEM((HEADS_PER_STEP, ATT_Q_TILE, LANES), F32),
            pltpu.VMEM((HEADS_PER_STEP, ATT_Q_TILE, 2 * HEAD_DIM), F32),
            pltpu.VMEM((HEADS_PER_STEP, ATT_Q_TILE, LANES), F32),
        ],
        compiler_params=_params("parallel", "arbitrary"),
        name="fox_prompt",
    )(q, k, v, c_tiles, cq_rows)


def _fox_sample_kernel(q_ref, kc_ref, vc_ref, kn_ref, vn_ref, cc_ref, cn_ref, o_ref, m_ref, acc_ref):
    j = pl.program_id(1)

    @pl.when(j == 0)
    def _():
        m_ref[...] = jnp.full_like(m_ref, NEG_INF)
        acc_ref[...] = jnp.zeros_like(acc_ref)

    c_new = cn_ref[...]
    c_pad = jnp.concatenate([c_new, jnp.zeros((LANES - N_HEADS, LANES), F32)], axis=0)
    cq_all = c_pad.T[:DEC_SEQ, :]

    def cq_of(h):
        return jnp.broadcast_to(cq_all[:, h:h + 1], (DEC_SEQ, LANES))

    for h in range(N_HEADS):
        cols = slice(h * HEAD_DIM, (h + 1) * HEAD_DIM)
        head_rows = pl.ds(h, CACHE_TILE, stride=N_HEADS)
        k = kc_ref[head_rows, :].astype(BF16)
        s = lax.dot_general(q_ref[:, cols], k, (((1,), (1,)), ((), ())), preferred_element_type=F32)
        logits = s + (_tile_lanes(cq_of(h), CACHE_TILE // LANES) - cc_ref[h:h + 1, :])
        _softmax_step(logits, vc_ref[head_rows, :].astype(BF16), m_ref.at[h], acc_ref.at[h])

    @pl.when(j == pl.num_programs(1) - 1)
    def _():
        row = lax.broadcasted_iota(jnp.int32, (DEC_SEQ, LANES), 0)
        col = lax.broadcasted_iota(jnp.int32, (DEC_SEQ, LANES), 1)
        pad = jnp.zeros((LANES - DEC_SEQ, HEAD_DIM), BF16)
        for h in range(N_HEADS):
            cols = slice(h * HEAD_DIM, (h + 1) * HEAD_DIM)
            k = jnp.concatenate([kn_ref[:, cols], pad], axis=0)
            v = jnp.concatenate([vn_ref[:, cols], pad], axis=0)
            s = lax.dot_general(q_ref[:, cols], k, (((1,), (1,)), ((), ())), preferred_element_type=F32)
            logits = s + (cq_of(h) - c_new[h:h + 1, :])
            logits = jnp.where(row >= col, logits, NEG_INF)
            _softmax_step(logits, v, m_ref.at[h], acc_ref.at[h])
            o_ref[:, cols] = _softmax_result(acc_ref.at[h]).astype(o_ref.dtype)


def _fox_sample(q, k_new, v_new, cache_k, cache_v, c_sample):
    first = SEQ // DEC_SEQ
    n_steps = PAST_LEN // CACHE_TILE
    new_spec = pl.BlockSpec((DEC_SEQ, D_MODEL), lambda b, j: (first + b, 0))
    cache_spec = pl.BlockSpec((None, CACHE_TILE * N_HEADS, HEAD_DIM), lambda b, j: (b, j, 0))
    cache_k = cache_k.reshape(DEC_BATCH, PAST_LEN * N_HEADS, HEAD_DIM)
    cache_v = cache_v.reshape(DEC_BATCH, PAST_LEN * N_HEADS, HEAD_DIM)
    return pl.pallas_call(
        _fox_sample_kernel,
        grid=(DEC_BATCH, n_steps),
        in_specs=[
            new_spec, cache_spec, cache_spec, new_spec, new_spec,
            pl.BlockSpec((None, N_HEADS, CACHE_TILE), lambda b, j: (b, 0, j)),
            pl.BlockSpec((None, N_HEADS, LANES), lambda b, j: (b, 0, PAST_LEN // LANES)),
        ],
        out_specs=pl.BlockSpec((DEC_SEQ, D_MODEL), lambda b, j: (b, 0)),
        out_shape=jax.ShapeDtypeStruct((N_SAMPLE, D_MODEL), BF16),
        scratch_shapes=[
            pltpu.VMEM((N_HEADS, DEC_SEQ, LANES), F32),
            pltpu.VMEM((N_HEADS, DEC_SEQ, 2 * HEAD_DIM), F32),
        ],
        compiler_params=_params("parallel", "arbitrary"),
        name="fox_sample",
    )(q, cache_k, cache_v, k_new, v_new, c_sample, c_sample)


def kernel(x_prompt, x_sample, cache_k, cache_v, cache_logf, ffn1_norm, ffn1_w_gate, ffn1_w_up,
           ffn1_w_down, mix_norm, ffn2_norm, ffn2_w_gate, ffn2_w_up, ffn2_w_down, gmlp_w_in,
           gmlp_ln_g, gmlp_ln_b, gmlp_w_s, gmlp_b_s, gmlp_w_out, kv_norm, w_k, w_v, w_f, b_f,
           fox_w_q, fox_w_o, final_norm):
    bf = lambda w: w.astype(BF16)
    x = lax.dynamic_update_slice(jnp.pad(x_prompt.reshape(SEQ, D_MODEL), ((0, N_SAMPLE), (0, 0))),
                                 x_sample.reshape(N_SAMPLE, D_MODEL), (SEQ, 0))

    ffn1 = (ffn1_w_gate, ffn1_w_up, ffn1_w_down)
    ffn2 = (ffn2_w_gate, ffn2_w_up, ffn2_w_down)
    ffn_bf = tuple(bf(w[0]) for w in ffn1)
    final_g = final_norm.reshape(1, D_MODEL)
    w_in_bf = bf(gmlp_w_in)

    reps = GMLP_CHUNK // DEC_SEQ
    ws2 = jnp.stack([gmlp_w_s, jnp.tile(gmlp_w_s[:, :, :DEC_SEQ, :DEC_SEQ], (1, 1, reps, reps))], axis=1)
    bs2 = jnp.stack([gmlp_b_s, jnp.tile(gmlp_b_s[:, :, :DEC_SEQ], (1, 1, reps))], axis=1)[..., None]

    gmlp_v = []
    logf_all = None
    for l in range(DEPTH):
        if l == N_A_LAYERS:
            kvg = kv_norm.reshape(1, D_MODEL)
            w_f_pad = jnp.pad(w_f, ((0, 0), (0, LANES - N_HEADS)))[None]
            b_f_pad = jnp.pad(b_f, (0, LANES - N_HEADS)).reshape(1, LANES)
            k_prompt, k_sample, k_bf = _norm_matmul(x, kvg, w_k[None], 0, SMALL_ROW_TILE, D_MODEL, None,
                                                    (BF16,), split=True)
            v_prompt, v_sample, v_bf, logf_pad = _norm_matmul(x, kvg, w_v[None], 0, SMALL_ROW_TILE, D_MODEL, None,
                                                              (BF16,), split=True, forget=(w_f_pad, b_f_pad))
            logf_all = logf_pad[:, :N_HEADS]
            c_prompt = _cumsum_lanes(logf_all[:SEQ].T)
            c_prompt = c_prompt.reshape(N_HEADS, SEQ // ATT_TILE, 1, ATT_TILE)
            lf_new = logf_all[SEQ:].reshape(DEC_BATCH, DEC_SEQ, N_HEADS).transpose(0, 2, 1)
            lf_cache = cache_logf.transpose(0, 2, 1)
            lf = jnp.concatenate(
                [lf_cache, lf_new, jnp.zeros((DEC_BATCH, N_HEADS, LANES - DEC_SEQ), F32)], axis=-1)
            c_sample = _cumsum_lanes(lf.reshape(DEC_BATCH * N_HEADS, PAST_LEN + LANES))
            c_sample = c_sample.reshape(DEC_BATCH, N_HEADS, PAST_LEN + LANES)

        x, *ffn_bf = _half_ffn(x, ffn1_norm[l].reshape(1, D_MODEL), ffn_bf, ffn2, l)
        mg = mix_norm[l].reshape(1, D_MODEL)
        if l < N_A_LAYERS:
            (z,) = _norm_matmul(x, mg, w_in_bf, l, ROW_TILE, 1024, "gelu", (F32,))
            x, vn = _gmlp_mix(z, x, gmlp_ln_g[l].reshape(1, D_GATE), gmlp_ln_b[l].reshape(1, D_GATE),
                              ws2[l], bs2[l], bf(gmlp_w_out[l]))
            gmlp_v.append(vn.reshape(DEC_BATCH, DEC_SEQ, D_GATE))
        else:
            jj = l - N_A_LAYERS
            (q,) = _norm_matmul(x, mg, fox_w_q, jj, SMALL_ROW_TILE, D_MODEL, None, (BF16,),
                                out_scale=HEAD_DIM ** -0.5 * LOG2E)
            o_prompt = _fox_prompt(q, k_bf, v_bf, c_prompt)
            o_sample = _fox_sample(q, k_bf, v_bf, cache_k, cache_v, c_sample)
            o = jnp.concatenate([o_prompt, o_sample], axis=0)
            x = _matmul_residual(o, fox_w_o, jj, x)
        g2 = ffn2_norm[l].reshape(1, D_MODEL)
        if l < DEPTH - 1:
            x, *ffn_bf = _half_ffn(x, g2, ffn_bf, ffn1, l + 1)
        else:
            x = _half_ffn(x, g2, ffn_bf, final_gain=final_g)

    y_prompt, y_sample = x
    hd = (N_HEADS, HEAD_DIM)
    return (y_prompt.reshape(1, SEQ, D_MODEL),
            y_sample.reshape(DEC_BATCH, DEC_SEQ, D_MODEL),
            k_prompt.reshape(1, SEQ, *hd),
            v_prompt.reshape(1, SEQ, *hd),
            logf_all[:SEQ].reshape(1, SEQ, N_HEADS),
            k_sample.reshape(DEC_BATCH, DEC_SEQ, *hd),
            v_sample.reshape(DEC_BATCH, DEC_SEQ, *hd),
            logf_all[SEQ:].reshape(DEC_BATCH, DEC_SEQ, N_HEADS),
            jnp.stack(gmlp_v, axis=0))
```

```python
import functools
import math

import jax
import jax.numpy as jnp
from jax import lax
from jax.experimental import pallas as pl
from jax.experimental.pallas import tpu as pltpu

D_MODEL = 2048
SEQ = 8192
DEPTH = 4
DEC_BATCH = 16
DEC_SEQ = 16
PAST_LEN = 2048
N_A_LAYERS = DEPTH // 2
D_FF = 5632
GMLP_CHUNK = 128
D_GATE = 2 * D_MODEL
GMLP_GROUPS = 4
GROUP_W = D_GATE // GMLP_GROUPS
N_HEADS = 16
HEAD_DIM = D_MODEL // N_HEADS
RMS_EPS = 1e-6
LN_EPS = 1e-5
NEG_INF = -1e30
LOG2E = math.log2(math.e)

N_SAMPLE = DEC_BATCH * DEC_SEQ
M_ROWS = SEQ + N_SAMPLE
LANES = 128

F32 = jnp.float32
BF16 = jnp.bfloat16

ROW_TILE = 768
SMALL_ROW_TILE = 384
FFN_ROW_TILE = 1056
FF_TILE = 512
FFN_CAST_TILE = 128
GMLP_ROW_TILE = 256
ATT_TILE = 512
ATT_Q_TILE = 512
HEADS_PER_STEP = 8
CACHE_TILE = 512
VMEM_LIMIT = 58 * 1024 * 1024


def _params(*sem):
    return pltpu.CompilerParams(dimension_semantics=sem, vmem_limit_bytes=VMEM_LIMIT)


def _rms(xf, g):
    y = xf * lax.rsqrt(jnp.mean(xf * xf, axis=-1, keepdims=True) + RMS_EPS)
    return y * g


def _ffn_kernel(*refs, last, f32_weights, split_input):
    x_ref, refs = refs[0], refs[1:]
    if split_input:
        xs_ref, refs = refs[0], refs[1:]
    g_ref, wg_ref, wu_ref, wd_ref = refs[:4]
    h_ref = refs[-1]
    i, j = pl.program_id(0), pl.program_id(1)
    last_i = pl.num_programs(0) - 1
    if last:
        gf_ref, o_ref, sample_ref = refs[4:7]
    else:
        next_refs, o_ref, cast_refs = refs[4:7], refs[7], refs[8:11]

        @pl.when(i * pl.num_programs(1) + j < D_FF // FFN_CAST_TILE)
        def _():
            for src, dst in zip(next_refs, cast_refs):
                dst[...] = src[...].astype(BF16)

    def with_rows(fn):
        if not split_input:
            fn(x_ref[...])
            return

        @pl.when(i < last_i)
        def _():
            fn(x_ref[...])

        @pl.when(i == last_i)
        def _():
            fn(jnp.concatenate([x_ref[:FFN_ROW_TILE - N_SAMPLE, :], xs_ref[...]], axis=0))

    @pl.when(j == 0)
    def _():
        def normalise(rows):
            h_ref[...] = _rms(rows, g_ref[...]).astype(BF16)

        with_rows(normalise)
        o_ref[...] = jnp.zeros_like(o_ref)

    h = h_ref[...]
    wg, wu, wd = wg_ref[...], wu_ref[...], wd_ref[...]
    if f32_weights:
        wg, wu, wd = wg.astype(BF16), wu.astype(BF16), wd.astype(BF16)
    a = jnp.dot(h, wg, preferred_element_type=F32)
    b = jnp.dot(h, wu, preferred_element_type=F32)
    act = (a * jax.nn.sigmoid(a) * b).astype(BF16)
    o_ref[...] += jnp.dot(act, wd, preferred_element_type=F32)

    @pl.when(j == pl.num_programs(1) - 1)
    def _():
        def residual(rows):
            r = rows + 0.5 * o_ref[...]
            if last:
                r = _rms(r, gf_ref[...])
            o_ref[...] = r
            if last:
                @pl.when(i == last_i)
                def _():
                    sample_ref[...] = r[FFN_ROW_TILE - N_SAMPLE:, :]

        with_rows(residual)


def _half_ffn(x, g, weights, next_weights=None, next_layer=None, final_gain=None, layer=None, x_sample=None):
    m = M_ROWS
    f32_weights = layer is not None
    ff_tile = FF_TILE // 2 if f32_weights else FF_TILE
    n_ff = D_FF // ff_tile
    grid = (m // FFN_ROW_TILE, n_ff)
    last = next_weights is None
    row_spec = pl.BlockSpec((FFN_ROW_TILE, D_MODEL), lambda i, j: (i, 0), pipeline_mode=pl.Buffered(1))
    in_specs = [pl.BlockSpec((FFN_ROW_TILE, D_MODEL), lambda i, j: (i, 0))]
    rows = (x,)
    if x_sample is not None:
        in_specs.append(pl.BlockSpec((N_SAMPLE, D_MODEL), lambda i, j: (0, 0)))
        rows = (x, x_sample)
    in_specs.append(pl.BlockSpec((1, D_MODEL), lambda i, j: (0, 0)))
    if f32_weights:
        in_specs += [
            pl.BlockSpec((None, D_MODEL, ff_tile), lambda i, j: (layer, 0, j)),
            pl.BlockSpec((None, D_MODEL, ff_tile), lambda i, j: (layer, 0, j)),
            pl.BlockSpec((None, ff_tile, D_MODEL), lambda i, j: (layer, j, 0)),
        ]
    else:
        in_specs += [
            pl.BlockSpec((D_MODEL, ff_tile), lambda i, j: (0, j)),
            pl.BlockSpec((D_MODEL, ff_tile), lambda i, j: (0, j)),
            pl.BlockSpec((ff_tile, D_MODEL), lambda i, j: (j, 0)),
        ]
    if last:
        in_specs.append(pl.BlockSpec((1, D_MODEL), lambda i, j: (0, 0)))
        operands = (*rows, g, *weights, final_gain)
        out_specs = [row_spec, pl.BlockSpec((N_SAMPLE, D_MODEL), lambda i, j: (0, 0))]
        out_shape = [jax.ShapeDtypeStruct((SEQ, D_MODEL), F32), jax.ShapeDtypeStruct((N_SAMPLE, D_MODEL), F32)]
    else:
        slab = lambda i, j: jnp.minimum(i * n_ff + j, D_FF // FFN_CAST_TILE - 1)
        in_specs += [
            pl.BlockSpec((None, D_MODEL, FFN_CAST_TILE), lambda i, j: (next_layer, 0, slab(i, j))),
            pl.BlockSpec((None, D_MODEL, FFN_CAST_TILE), lambda i, j: (next_layer, 0, slab(i, j))),
            pl.BlockSpec((None, FFN_CAST_TILE, D_MODEL), lambda i, j: (next_layer, slab(i, j), 0)),
        ]
        operands = (*rows, g, *weights, *next_weights)
        out_specs = [
            row_spec,
            pl.BlockSpec((D_MODEL, FFN_CAST_TILE), lambda i, j: (0, slab(i, j))),
            pl.BlockSpec((D_MODEL, FFN_CAST_TILE), lambda i, j: (0, slab(i, j))),
            pl.BlockSpec((FFN_CAST_TILE, D_MODEL), lambda i, j: (slab(i, j), 0)),
        ]
        out_shape = [
            jax.ShapeDtypeStruct((m, D_MODEL), F32),
            jax.ShapeDtypeStruct((D_MODEL, D_FF), BF16),
            jax.ShapeDtypeStruct((D_MODEL, D_FF), BF16),
            jax.ShapeDtypeStruct((D_FF, D_MODEL), BF16),
        ]
    return pl.pallas_call(
        functools.partial(_ffn_kernel, last=last, f32_weights=f32_weights, split_input=x_sample is not None),
        grid=grid,
        in_specs=in_specs,
        out_specs=out_specs,
        out_shape=out_shape,
        scratch_shapes=[pltpu.VMEM((FFN_ROW_TILE, D_MODEL), BF16)],
        compiler_params=_params("arbitrary", "arbitrary"),
        name="half_ffn",
    )(*operands)


def _gelu(z):
    return 0.5 * z * (1.0 + lax.erf(z * (1.0 / math.sqrt(2.0))))


def _log_sigmoid(z):
    return -(jnp.maximum(-z, 0.0) + jnp.log1p(jnp.exp(-jnp.abs(z))))


def _norm_matmul_kernel(x_ref, g_ref, w_ref, *rest, act, n_out, out_scale, split, row_tile, forget):
    if forget:
        wf_ref, bf_ref, rest = rest[0], rest[1], rest[2:]
    outs, h_ref = rest[:n_out], rest[n_out]
    j = pl.program_id(1)

    @pl.when(j == 0)
    def _():
        h_ref[...] = _rms(x_ref[...], g_ref[...]).astype(BF16)

    z = jnp.dot(h_ref[...], w_ref[...].astype(BF16), preferred_element_type=F32)
    if act == "gelu":
        z = _gelu(z)
    if forget:
        zf = jnp.dot(h_ref[...], wf_ref[...].astype(BF16), preferred_element_type=F32)
        outs[-1][...] = _log_sigmoid(zf + bf_ref[...])
        outs = outs[:-1]
    if out_scale != 1.0:
        z = z * out_scale
    if split:
        outs[0][...] = z

        @pl.when(pl.program_id(0) == pl.num_programs(0) - 1)
        def _():
            outs[1][...] = z[row_tile - N_SAMPLE:, :]
        outs = outs[2:]
    for o in outs:
        o[...] = z.astype(o.dtype)


def _norm_matmul(x, g, w, layer, row_tile, col_tile, act, out_dtypes, out_scale=1.0, split=False, forget=None):
    m = x.shape[0]
    n = w.shape[2]
    grid = (m // row_tile, n // col_tile)
    assert forget is None or col_tile == n
    w_mode = dict(pipeline_mode=pl.Buffered(1)) if col_tile == n else {}
    out_spec = pl.BlockSpec((row_tile, col_tile), lambda i, j: (i, j))
    out_specs = [out_spec] * len(out_dtypes)
    out_shape = [jax.ShapeDtypeStruct((m, n), d) for d in out_dtypes]
    if split:
        last = grid[0] - 1
        sample_spec = pl.BlockSpec((N_SAMPLE, col_tile), lambda i, j: (0, jnp.where(i == last, j, 0)))
        out_specs = [out_spec, sample_spec] + out_specs
        out_shape = [jax.ShapeDtypeStruct((SEQ, n), F32), jax.ShapeDtypeStruct((N_SAMPLE, n), F32)] + out_shape
    in_specs = [
        pl.BlockSpec((row_tile, D_MODEL), lambda i, j: (i, 0)),
        pl.BlockSpec((1, D_MODEL), lambda i, j: (0, 0)),
        pl.BlockSpec((None, D_MODEL, col_tile), lambda i, j: (layer, 0, j), **w_mode),
    ]
    operands = [x, g, w]
    if forget is not None:
        in_specs += [pl.BlockSpec((None, D_MODEL, LANES), lambda i, j: (0, 0, 0)),
                     pl.BlockSpec((1, LANES), lambda i, j: (0, 0))]
        operands += list(forget)
        out_specs = out_specs + [pl.BlockSpec((row_tile, LANES), lambda i, j: (i, 0))]
        out_shape = out_shape + [jax.ShapeDtypeStruct((m, LANES), F32)]
    return pl.pallas_call(
        functools.partial(_norm_matmul_kernel, act=act, n_out=len(out_shape), out_scale=out_scale,
                          split=split, row_tile=row_tile, forget=forget is not None),
        grid=grid,
        in_specs=in_specs,
        out_specs=out_specs,
        out_shape=out_shape,
        scratch_shapes=[pltpu.VMEM((row_tile, D_MODEL), BF16)],
        compiler_params=_params("arbitrary" if split else "parallel", "arbitrary"),
        name="norm_matmul_" + str(act),
    )(*operands)


def _matmul_residual_kernel(ap_ref, as_ref, w_ref, x_ref, o_ref):
    i, last_i = pl.program_id(0), pl.num_programs(0) - 1
    w = w_ref[...].astype(BF16)

    @pl.when(i < last_i)
    def _():
        o_ref[...] = x_ref[...] + jnp.dot(ap_ref[...], w, preferred_element_type=F32)

    @pl.when(i == last_i)
    def _():
        a = jnp.concatenate([ap_ref[:SMALL_ROW_TILE - N_SAMPLE, :], as_ref[...]], axis=0)
        o_ref[...] = x_ref[...] + jnp.dot(a, w, preferred_element_type=F32)


def _matmul_residual(a_prompt, a_sample, w, layer, x):
    m = x.shape[0]
    k, n = w.shape[1:]
    return pl.pallas_call(
        _matmul_residual_kernel,
        grid=(m // SMALL_ROW_TILE,),
        in_specs=[
            pl.BlockSpec((SMALL_ROW_TILE, k), lambda i: (i, 0)),
            pl.BlockSpec((N_SAMPLE, k), lambda i: (0, 0)),
            pl.BlockSpec((None, k, n), lambda i: (layer, 0, 0), pipeline_mode=pl.Buffered(1)),
            pl.BlockSpec((SMALL_ROW_TILE, n), lambda i: (i, 0)),
        ],
        out_specs=pl.BlockSpec((SMALL_ROW_TILE, n), lambda i: (i, 0)),
        out_shape=jax.ShapeDtypeStruct((m, n), F32),
        compiler_params=_params("parallel"),
        name="matmul_residual",
    )(a_prompt, a_sample, w, x)


def _gmlp_mix_kernel(u_ref, v_ref, x_ref, lng_ref, lnb_ref, ws_ref, bs_ref, wo_ref,
                     o_ref, vn_ref, gated_ref, *, n_prompt_tiles):
    i = pl.program_id(0)
    v = v_ref[...]
    mu = jnp.mean(v, axis=-1, keepdims=True)
    vc = v - mu
    vn = vc * lax.rsqrt(jnp.mean(vc * vc, axis=-1, keepdims=True) + LN_EPS)
    vn = vn * lng_ref[...] + lnb_ref[...]

    is_sample = i >= n_prompt_tiles

    @pl.when(is_sample)
    def _():
        vn_ref[...] = vn

    sel = is_sample.astype(jnp.int32)
    shift = jnp.where(is_sample, int(math.log2(DEC_SEQ)), int(math.log2(GMLP_CHUNK)))
    t = lax.broadcasted_iota(jnp.int32, (GMLP_CHUNK, GMLP_CHUNK), 0)
    s = lax.broadcasted_iota(jnp.int32, (GMLP_CHUNK, GMLP_CHUNK), 1)
    mask = (s <= t) & ((t >> shift) == (s >> shift))
    vnb = vn.astype(BF16)
    for grp in range(GMLP_GROUPS):
        w = jnp.where(mask, ws_ref[sel, grp], 0.0).astype(BF16)
        bias = bs_ref[sel, grp]
        cols = slice(grp * GROUP_W, (grp + 1) * GROUP_W)
        for c in range(GMLP_ROW_TILE // GMLP_CHUNK):
            rows = slice(c * GMLP_CHUNK, (c + 1) * GMLP_CHUNK)
            mixed = jnp.dot(w, vnb[rows, cols], preferred_element_type=F32) + bias
            gated_ref[rows, cols] = (u_ref[rows, cols] * mixed).astype(BF16)
    o_ref[...] = x_ref[...] + jnp.dot(gated_ref[...], wo_ref[...], preferred_element_type=F32)


def _gmlp_mix(z, x, ln_g, ln_b, ws2, bs2, w_out):
    m = x.shape[0]
    n_prompt_tiles = SEQ // GMLP_ROW_TILE
    single = pl.Buffered(1)
    return pl.pallas_call(
        functools.partial(_gmlp_mix_kernel, n_prompt_tiles=n_prompt_tiles),
        grid=(m // GMLP_ROW_TILE,),
        in_specs=[
            pl.BlockSpec((GMLP_ROW_TILE, D_GATE), lambda i: (i, 0)),
            pl.BlockSpec((GMLP_ROW_TILE, D_GATE), lambda i: (i, 1)),
            pl.BlockSpec((GMLP_ROW_TILE, D_MODEL), lambda i: (i, 0)),
            pl.BlockSpec((1, D_GATE), lambda i: (0, 0)),
            pl.BlockSpec((1, D_GATE), lambda i: (0, 0)),
            pl.BlockSpec((2, GMLP_GROUPS, GMLP_CHUNK, GMLP_CHUNK), lambda i: (0, 0, 0, 0)),
            pl.BlockSpec((2, GMLP_GROUPS, GMLP_CHUNK, 1), lambda i: (0, 0, 0, 0)),
            pl.BlockSpec((D_GATE, D_MODEL), lambda i: (0, 0), pipeline_mode=single),
        ],
        out_specs=[
            pl.BlockSpec((GMLP_ROW_TILE, D_MODEL), lambda i: (i, 0)),
            pl.BlockSpec((GMLP_ROW_TILE, D_GATE),
                         lambda i: (jnp.maximum(i - n_prompt_tiles, 0), 0), pipeline_mode=single),
        ],
        out_shape=[
            jax.ShapeDtypeStruct((m, D_MODEL), F32),
            jax.ShapeDtypeStruct((N_SAMPLE, D_GATE), F32),
        ],
        scratch_shapes=[pltpu.VMEM((GMLP_ROW_TILE, D_GATE), BF16)],
        compiler_params=_params("arbitrary"),
        name="gmlp_mix",
    )(z, z, x, ln_g, ln_b, ws2, bs2, w_out)


def _scan_kernel(x_ref, o_ref):
    x = x_ref[...]
    n = x.shape[-1]
    idx = lax.broadcasted_iota(jnp.int32, x.shape, 1)
    step = 1
    while step < n:
        x = x + jnp.where(idx >= step, pltpu.roll(x, step, axis=1), 0.0)
        step *= 2
    o_ref[...] = x * LOG2E


def _cumsum_lanes(x):
    return pl.pallas_call(
        _scan_kernel,
        out_shape=jax.ShapeDtypeStruct(x.shape, F32),
        compiler_params=pltpu.CompilerParams(vmem_limit_bytes=VMEM_LIMIT),
        name="cumsum_lanes",
    )(x)


def _tile_lanes(x, reps):
    return jnp.concatenate([x] * reps, axis=1)


def _softmax_step(logits, v, m_ref, acc_ref):
    reps = logits.shape[1] // LANES
    m_prev = m_ref[...]
    m_new = jnp.maximum(m_prev, jnp.max(logits, axis=1, keepdims=True))
    p = jnp.exp2(logits - _tile_lanes(m_new, reps))
    alpha = jnp.exp2(m_prev - m_new)
    v_ones = jnp.concatenate([v, jnp.ones_like(v)], axis=1)
    pv = jnp.dot(p.astype(BF16), v_ones, preferred_element_type=F32)
    acc_ref[...] = _tile_lanes(alpha, 2) * acc_ref[...] + pv
    m_ref[...] = m_new


def _softmax_result(acc_ref):
    acc = acc_ref[...]
    return acc[:, :HEAD_DIM] / acc[:, HEAD_DIM:]


def _fox_prompt_kernel(q_ref, k_ref, v_ref, c_ref, cq_rows_ref, o_ref, m_ref, acc_ref, cq_ref):
    qi = pl.program_id(1)
    per_key_tile = ATT_TILE // ATT_Q_TILE
    diag = qi // per_key_tile
    first_row = (qi % per_key_tile) * ATT_Q_TILE
    for hh in range(HEADS_PER_STEP):
        cq_ref[hh] = jnp.broadcast_to(cq_rows_ref[hh, qi], (LANES, ATT_Q_TILE)).T
    m_ref[...] = jnp.full_like(m_ref, NEG_INF)
    acc_ref[...] = jnp.zeros_like(acc_ref)

    def block(kj, on_diagonal):
        start = pl.multiple_of(kj * ATT_TILE, ATT_TILE)
        for hh in range(HEADS_PER_STEP):
            cols = slice(hh * HEAD_DIM, (hh + 1) * HEAD_DIM)
            k = k_ref[pl.ds(start, ATT_TILE), cols]
            s = lax.dot_general(q_ref[:, cols], k, (((1,), (1,)), ((), ())), preferred_element_type=F32)
            logits = s + (_tile_lanes(cq_ref[hh], ATT_TILE // LANES) - c_ref[hh, kj])
            if on_diagonal:
                row = lax.broadcasted_iota(jnp.int32, (ATT_Q_TILE, ATT_TILE), 0) + first_row
                col = lax.broadcasted_iota(jnp.int32, (ATT_Q_TILE, ATT_TILE), 1)
                logits = jnp.where(row >= col, logits, NEG_INF)
            _softmax_step(logits, v_ref[pl.ds(start, ATT_TILE), cols], m_ref.at[hh], acc_ref.at[hh])

    def body(kj, carry):
        block(kj, False)
        return carry

    lax.fori_loop(0, diag, body, 0)
    block(diag, True)
    for hh in range(HEADS_PER_STEP):
        cols = slice(hh * HEAD_DIM, (hh + 1) * HEAD_DIM)
        o_ref[:, cols] = _softmax_result(acc_ref.at[hh]).astype(o_ref.dtype)


def _fox_prompt(q, k, v, c_tiles):
    n_tiles = SEQ // ATT_TILE
    n_q_tiles = SEQ // ATT_Q_TILE
    width = HEADS_PER_STEP * HEAD_DIM
    cq_rows = c_tiles.reshape(N_HEADS, n_q_tiles, 1, ATT_Q_TILE)
    return pl.pallas_call(
        _fox_prompt_kernel,
        grid=(N_HEADS // HEADS_PER_STEP, n_q_tiles),
        in_specs=[
            pl.BlockSpec((ATT_Q_TILE, width), lambda h, i: (i, h)),
            pl.BlockSpec((SEQ, width), lambda h, i: (0, h), pipeline_mode=pl.Buffered(1)),
            pl.BlockSpec((SEQ, width), lambda h, i: (0, h), pipeline_mode=pl.Buffered(1)),
            pl.BlockSpec((HEADS_PER_STEP, n_tiles, 1, ATT_TILE), lambda h, i: (h, 0, 0, 0)),
            pl.BlockSpec((HEADS_PER_STEP, n_q_tiles, 1, ATT_Q_TILE), lambda h, i: (h, 0, 0, 0)),
        ],
        out_specs=pl.BlockSpec((ATT_Q_TILE, width), lambda h, i: (i, h)),
        out_shape=jax.ShapeDtypeStruct((SEQ, D_MODEL), BF16),
        scratch_shapes=[
            pltpu.VMEM((HEADS_PER_STEP, ATT_Q_TILE, LANES), F32),
            pltpu.VMEM((HEADS_PER_STEP, ATT_Q_TILE, 2 * HEAD_DIM), F32),
            pltpu.VMEM((HEADS_PER_STEP, ATT_Q_TILE, LANES), F32),
        ],
        compiler_params=_params("parallel", "arbitrary"),
        name="fox_prompt",
    )(q, k, v, c_tiles, cq_rows)


def _fox_sample_kernel(q_ref, kc_ref, vc_ref, kn_ref, vn_ref, cc_ref, cn_ref, o_ref, m_ref, acc_ref):
    j = pl.program_id(1)

    @pl.when(j == 0)
    def _():
        m_ref[...] = jnp.full_like(m_ref, NEG_INF)
        acc_ref[...] = jnp.zeros_like(acc_ref)

    c_new = cn_ref[...]
    c_pad = jnp.concatenate([c_new, jnp.zeros((LANES - N_HEADS, LANES), F32)], axis=0)
    cq_all = c_pad.T[:DEC_SEQ, :]

    def cq_of(h):
        return jnp.broadcast_to(cq_all[:, h:h + 1], (DEC_SEQ, LANES))

    for h in range(N_HEADS):
        cols = slice(h * HEAD_DIM, (h + 1) * HEAD_DIM)
        head_rows = pl.ds(h, CACHE_TILE, stride=N_HEADS)
        k = kc_ref[head_rows, :].astype(BF16)
        s = lax.dot_general(q_ref[:, cols], k, (((1,), (1,)), ((), ())), preferred_element_type=F32)
        logits = s + (_tile_lanes(cq_of(h), CACHE_TILE // LANES) - cc_ref[h:h + 1, :])
        _softmax_step(logits, vc_ref[head_rows, :].astype(BF16), m_ref.at[h], acc_ref.at[h])

    @pl.when(j == pl.num_programs(1) - 1)
    def _():
        row = lax.broadcasted_iota(jnp.int32, (DEC_SEQ, LANES), 0)
        col = lax.broadcasted_iota(jnp.int32, (DEC_SEQ, LANES), 1)
        pad = jnp.zeros((LANES - DEC_SEQ, HEAD_DIM), BF16)
        for h in range(N_HEADS):
            cols = slice(h * HEAD_DIM, (h + 1) * HEAD_DIM)
            k = jnp.concatenate([kn_ref[:, cols], pad], axis=0)
            v = jnp.concatenate([vn_ref[:, cols], pad], axis=0)
            s = lax.dot_general(q_ref[:, cols], k, (((1,), (1,)), ((), ())), preferred_element_type=F32)
            logits = s + (cq_of(h) - c_new[h:h + 1, :])
            logits = jnp.where(row >= col, logits, NEG_INF)
            _softmax_step(logits, v, m_ref.at[h], acc_ref.at[h])
            o_ref[:, cols] = _softmax_result(acc_ref.at[h]).astype(o_ref.dtype)


def _fox_sample(q, k_new, v_new, cache_k, cache_v, c_sample):
    first = SEQ // DEC_SEQ
    n_steps = PAST_LEN // CACHE_TILE
    new_spec = pl.BlockSpec((DEC_SEQ, D_MODEL), lambda b, j: (first + b, 0))
    cache_spec = pl.BlockSpec((None, CACHE_TILE * N_HEADS, HEAD_DIM), lambda b, j: (b, j, 0))
    cache_k = cache_k.reshape(DEC_BATCH, PAST_LEN * N_HEADS, HEAD_DIM)
    cache_v = cache_v.reshape(DEC_BATCH, PAST_LEN * N_HEADS, HEAD_DIM)
    return pl.pallas_call(
        _fox_sample_kernel,
        grid=(DEC_BATCH, n_steps),
        in_specs=[
            new_spec, cache_spec, cache_spec, new_spec, new_spec,
            pl.BlockSpec((None, N_HEADS, CACHE_TILE), lambda b, j: (b, 0, j)),
            pl.BlockSpec((None, N_HEADS, LANES), lambda b, j: (b, 0, PAST_LEN // LANES)),
        ],
        out_specs=pl.BlockSpec((DEC_SEQ, D_MODEL), lambda b, j: (b, 0)),
        out_shape=jax.ShapeDtypeStruct((N_SAMPLE, D_MODEL), BF16),
        scratch_shapes=[
            pltpu.VMEM((N_HEADS, DEC_SEQ, LANES), F32),
            pltpu.VMEM((N_HEADS, DEC_SEQ, 2 * HEAD_DIM), F32),
        ],
        compiler_params=_params("parallel", "arbitrary"),
        name="fox_sample",
    )(q, cache_k, cache_v, k_new, v_new, c_sample, c_sample)


def kernel(x_prompt, x_sample, cache_k, cache_v, cache_logf, ffn1_norm, ffn1_w_gate, ffn1_w_up,
           ffn1_w_down, mix_norm, ffn2_norm, ffn2_w_gate, ffn2_w_up, ffn2_w_down, gmlp_w_in,
           gmlp_ln_g, gmlp_ln_b, gmlp_w_s, gmlp_b_s, gmlp_w_out, kv_norm, w_k, w_v, w_f, b_f,
           fox_w_q, fox_w_o, final_norm):
    bf = lambda w: w.astype(BF16)
    x = x_prompt.reshape(SEQ, D_MODEL)

    ffn1 = (ffn1_w_gate, ffn1_w_up, ffn1_w_down)
    ffn2 = (ffn2_w_gate, ffn2_w_up, ffn2_w_down)
    final_g = final_norm.reshape(1, D_MODEL)
    w_in_bf = bf(gmlp_w_in)

    reps = GMLP_CHUNK // DEC_SEQ
    ws2 = jnp.stack([gmlp_w_s, jnp.tile(gmlp_w_s[:, :, :DEC_SEQ, :DEC_SEQ], (1, 1, reps, reps))], axis=1)
    bs2 = jnp.stack([gmlp_b_s, jnp.tile(gmlp_b_s[:, :, :DEC_SEQ], (1, 1, reps))], axis=1)[..., None]

    gmlp_v = []
    logf_all = None
    for l in range(DEPTH):
        if l == N_A_LAYERS:
            kvg = kv_norm.reshape(1, D_MODEL)
            w_f_pad = jnp.pad(w_f, ((0, 0), (0, LANES - N_HEADS)))[None]
            b_f_pad = jnp.pad(b_f, (0, LANES - N_HEADS)).reshape(1, LANES)
            k_prompt, k_sample, k_bf = _norm_matmul(x, kvg, w_k[None], 0, SMALL_ROW_TILE, D_MODEL, None,
                                                    (BF16,), split=True)
            v_prompt, v_sample, v_bf, logf_pad = _norm_matmul(x, kvg, w_v[None], 0, SMALL_ROW_TILE, D_MODEL, None,
                                                              (BF16,), split=True, forget=(w_f_pad, b_f_pad))
            logf_all = logf_pad[:, :N_HEADS]
            c_prompt = _cumsum_lanes(logf_all[:SEQ].T)
            c_prompt = c_prompt.reshape(N_HEADS, SEQ // ATT_TILE, 1, ATT_TILE)
            lf_new = logf_all[SEQ:].reshape(DEC_BATCH, DEC_SEQ, N_HEADS).transpose(0, 2, 1)
            lf_cache = cache_logf.transpose(0, 2, 1)
            lf = jnp.concatenate(
                [lf_cache, lf_new, jnp.zeros((DEC_BATCH, N_HEADS, LANES - DEC_SEQ), F32)], axis=-1)
            c_sample = _cumsum_lanes(lf.reshape(DEC_BATCH * N_HEADS, PAST_LEN + LANES))
            c_sample = c_sample.reshape(DEC_BATCH, N_HEADS, PAST_LEN + LANES)

        first = (dict(weights=ffn1, layer=0, x_sample=x_sample.reshape(N_SAMPLE, D_MODEL)) if l == 0
                 else dict(weights=ffn_bf))
        x, *ffn_bf = _half_ffn(x, ffn1_norm[l].reshape(1, D_MODEL), next_weights=ffn2, next_layer=l, **first)
        mg = mix_norm[l].reshape(1, D_MODEL)
        if l < N_A_LAYERS:
            (z,) = _norm_matmul(x, mg, w_in_bf, l, ROW_TILE, 1024, "gelu", (F32,))
            x, vn = _gmlp_mix(z, x, gmlp_ln_g[l].reshape(1, D_GATE), gmlp_ln_b[l].reshape(1, D_GATE),
                              ws2[l], bs2[l], bf(gmlp_w_out[l]))
            gmlp_v.append(vn.reshape(DEC_BATCH, DEC_SEQ, D_GATE))
        else:
            jj = l - N_A_LAYERS
            (q,) = _norm_matmul(x, mg, fox_w_q, jj, SMALL_ROW_TILE, D_MODEL, None, (BF16,),
                                out_scale=HEAD_DIM ** -0.5 * LOG2E)
            o_prompt = _fox_prompt(q, k_bf, v_bf, c_prompt)
            o_sample = _fox_sample(q, k_bf, v_bf, cache_k, cache_v, c_sample)
            x = _matmul_residual(o_prompt, o_sample, fox_w_o, jj, x)
        g2 = ffn2_norm[l].reshape(1, D_MODEL)
        if l < DEPTH - 1:
            x, *ffn_bf = _half_ffn(x, g2, ffn_bf, ffn1, l + 1)
        else:
            x = _half_ffn(x, g2, ffn_bf, final_gain=final_g)

    y_prompt, y_sample = x
    hd = (N_HEADS, HEAD_DIM)
    return (y_prompt.reshape(1, SEQ, D_MODEL),
            y_sample.reshape(DEC_BATCH, DEC_SEQ, D_MODEL),
            k_prompt.reshape(1, SEQ, *hd),
            v_prompt.reshape(1, SEQ, *hd),
            logf_all[:SEQ].reshape(1, SEQ, N_HEADS),
            k_sample.reshape(DEC_BATCH, DEC_SEQ, *hd),
            v_sample.reshape(DEC_BATCH, DEC_SEQ, *hd),
            logf_all[SEQ:].reshape(DEC_BATCH, DEC_SEQ, N_HEADS),
            jnp.stack(gmlp_v, axis=0))
```

```python
import functools
import math

import jax
import jax.numpy as jnp
from jax import lax
from jax.experimental import pallas as pl
from jax.experimental.pallas import tpu as pltpu

D_MODEL = 2048
SEQ = 8192
DEPTH = 4
DEC_BATCH = 16
DEC_SEQ = 16
PAST_LEN = 2048
N_A_LAYERS = DEPTH // 2
D_FF = 5632
GMLP_CHUNK = 128
D_GATE = 2 * D_MODEL
GMLP_GROUPS = 4
GROUP_W = D_GATE // GMLP_GROUPS
N_HEADS = 16
HEAD_DIM = D_MODEL // N_HEADS
RMS_EPS = 1e-6
LN_EPS = 1e-5
NEG_INF = -1e30
LOG2E = math.log2(math.e)

N_SAMPLE = DEC_BATCH * DEC_SEQ
M_ROWS = SEQ + N_SAMPLE
LANES = 128

F32 = jnp.float32
BF16 = jnp.bfloat16

ROW_TILE = 768
SMALL_ROW_TILE = 384
FFN_ROW_TILE = 1056
FF_TILE = 512
FFN_CAST_TILE = 128
GMLP_ROW_TILE = 256
ATT_TILE = 512
ATT_Q_TILE = 512
HEADS_PER_STEP = 8
CACHE_TILE = 512
VMEM_LIMIT = 58 * 1024 * 1024


def _params(*sem):
    return pltpu.CompilerParams(dimension_semantics=sem, vmem_limit_bytes=VMEM_LIMIT)


def _rms(xf, g):
    y = xf * lax.rsqrt(jnp.mean(xf * xf, axis=-1, keepdims=True) + RMS_EPS)
    return y * g


def _ffn_kernel(*refs, last, f32_weights, split_input):
    x_ref, refs = refs[0], refs[1:]
    if split_input:
        xs_ref, refs = refs[0], refs[1:]
    g_ref, wg_ref, wu_ref, wd_ref = refs[:4]
    h_ref = refs[-1]
    i, j = pl.program_id(0), pl.program_id(1)
    last_i = pl.num_programs(0) - 1
    if last:
        gf_ref, o_ref, sample_ref = refs[4:7]
    else:
        next_refs, o_ref, cast_refs = refs[4:7], refs[7], refs[8:11]

        @pl.when(i * pl.num_programs(1) + j < D_FF // FFN_CAST_TILE)
        def _():
            for src, dst in zip(next_refs, cast_refs):
                dst[...] = src[...].astype(BF16)

    def with_rows(fn):
        if not split_input:
            fn(x_ref[...])
            return

        @pl.when(i < last_i)
        def _():
            fn(x_ref[...])

        @pl.when(i == last_i)
        def _():
            fn(jnp.concatenate([x_ref[:FFN_ROW_TILE - N_SAMPLE, :], xs_ref[...]], axis=0))

    @pl.when(j == 0)
    def _():
        def normalise(rows):
            h_ref[...] = _rms(rows, g_ref[...]).astype(BF16)

        with_rows(normalise)
        o_ref[...] = jnp.zeros_like(o_ref)

    h = h_ref[...]
    wg, wu, wd = wg_ref[...], wu_ref[...], wd_ref[...]
    if f32_weights:
        wg, wu, wd = wg.astype(BF16), wu.astype(BF16), wd.astype(BF16)
    a = jnp.dot(h, wg, preferred_element_type=F32)
    b = jnp.dot(h, wu, preferred_element_type=F32)
    act = (a * jax.nn.sigmoid(a) * b).astype(BF16)
    o_ref[...] += jnp.dot(act, wd, preferred_element_type=F32)

    @pl.when(j == pl.num_programs(1) - 1)
    def _():
        def residual(rows):
            r = rows + 0.5 * o_ref[...]
            if last:
                r = _rms(r, gf_ref[...])
            o_ref[...] = r
            if last:
                @pl.when(i == last_i)
                def _():
                    sample_ref[...] = r[FFN_ROW_TILE - N_SAMPLE:, :]

        with_rows(residual)


def _half_ffn(x, g, weights, next_weights=None, next_layer=None, final_gain=None, layer=None, x_sample=None):
    m = M_ROWS
    f32_weights = layer is not None
    ff_tile = FF_TILE // 2 if f32_weights else FF_TILE
    n_ff = D_FF // ff_tile
    grid = (m // FFN_ROW_TILE, n_ff)
    last = next_weights is None
    row_spec = pl.BlockSpec((FFN_ROW_TILE, D_MODEL), lambda i, j: (i, 0), pipeline_mode=pl.Buffered(1))
    in_specs = [pl.BlockSpec((FFN_ROW_TILE, D_MODEL), lambda i, j: (i, 0))]
    rows = (x,)
    if x_sample is not None:
        in_specs.append(pl.BlockSpec((N_SAMPLE, D_MODEL), lambda i, j: (0, 0)))
        rows = (x, x_sample)
    in_specs.append(pl.BlockSpec((1, D_MODEL), lambda i, j: (0, 0)))
    if f32_weights:
        in_specs += [
            pl.BlockSpec((None, D_MODEL, ff_tile), lambda i, j: (layer, 0, j)),
            pl.BlockSpec((None, D_MODEL, ff_tile), lambda i, j: (layer, 0, j)),
            pl.BlockSpec((None, ff_tile, D_MODEL), lambda i, j: (layer, j, 0)),
        ]
    else:
        in_specs += [
            pl.BlockSpec((D_MODEL, ff_tile), lambda i, j: (0, j)),
            pl.BlockSpec((D_MODEL, ff_tile), lambda i, j: (0, j)),
            pl.BlockSpec((ff_tile, D_MODEL), lambda i, j: (j, 0)),
        ]
    if last:
        in_specs.append(pl.BlockSpec((1, D_MODEL), lambda i, j: (0, 0)))
        operands = (*rows, g, *weights, final_gain)
        out_specs = [row_spec, pl.BlockSpec((N_SAMPLE, D_MODEL), lambda i, j: (0, 0))]
        out_shape = [jax.ShapeDtypeStruct((SEQ, D_MODEL), F32), jax.ShapeDtypeStruct((N_SAMPLE, D_MODEL), F32)]
    else:
        slab = lambda i, j: jnp.minimum(i * n_ff + j, D_FF // FFN_CAST_TILE - 1)
        in_specs += [
            pl.BlockSpec((None, D_MODEL, FFN_CAST_TILE), lambda i, j: (next_layer, 0, slab(i, j))),
            pl.BlockSpec((None, D_MODEL, FFN_CAST_TILE), lambda i, j: (next_layer, 0, slab(i, j))),
            pl.BlockSpec((None, FFN_CAST_TILE, D_MODEL), lambda i, j: (next_layer, slab(i, j), 0)),
        ]
        operands = (*rows, g, *weights, *next_weights)
        out_specs = [
            row_spec,
            pl.BlockSpec((D_MODEL, FFN_CAST_TILE), lambda i, j: (0, slab(i, j))),
            pl.BlockSpec((D_MODEL, FFN_CAST_TILE), lambda i, j: (0, slab(i, j))),
            pl.BlockSpec((FFN_CAST_TILE, D_MODEL), lambda i, j: (slab(i, j), 0)),
        ]
        out_shape = [
            jax.ShapeDtypeStruct((m, D_MODEL), F32),
            jax.ShapeDtypeStruct((D_MODEL, D_FF), BF16),
            jax.ShapeDtypeStruct((D_MODEL, D_FF), BF16),
            jax.ShapeDtypeStruct((D_FF, D_MODEL), BF16),
        ]
    return pl.pallas_call(
        functools.partial(_ffn_kernel, last=last, f32_weights=f32_weights, split_input=x_sample is not None),
        grid=grid,
        in_specs=in_specs,
        out_specs=out_specs,
        out_shape=out_shape,
        scratch_shapes=[pltpu.VMEM((FFN_ROW_TILE, D_MODEL), BF16)],
        compiler_params=_params("arbitrary", "arbitrary"),
        name="half_ffn",
    )(*operands)


def _gelu(z):
    return 0.5 * z * (1.0 + lax.erf(z * (1.0 / math.sqrt(2.0))))


def _log_sigmoid(z):
    return -(jnp.maximum(-z, 0.0) + jnp.log1p(jnp.exp(-jnp.abs(z))))


def _norm_matmul_kernel(x_ref, g_ref, w_ref, *rest, act, n_out, out_scale, split, row_tile, forget, cast_slabs):
    if forget:
        wf_ref, bf_ref, rest = rest[0], rest[1], rest[2:]
    n_cast = len(cast_slabs)
    cast_src, rest = rest[:n_cast], rest[n_cast:]
    outs, h_ref = rest[:n_out], rest[n_out]
    outs, cast_dst = outs[:n_out - n_cast], outs[n_out - n_cast:]
    j = pl.program_id(1)
    for src, dst, n_slabs in zip(cast_src, cast_dst, cast_slabs):
        @pl.when(pl.program_id(0) * pl.num_programs(1) + j < n_slabs)
        def _(src=src, dst=dst):
            dst[...] = src[...].astype(BF16)

    @pl.when(j == 0)
    def _():
        h_ref[...] = _rms(x_ref[...], g_ref[...]).astype(BF16)

    z = jnp.dot(h_ref[...], w_ref[...].astype(BF16), preferred_element_type=F32)
    if act == "gelu":
        z = _gelu(z)
    if forget:
        zf = jnp.dot(h_ref[...], wf_ref[...].astype(BF16), preferred_element_type=F32)
        outs[-1][...] = _log_sigmoid(zf + bf_ref[...])
        outs = outs[:-1]
    if out_scale != 1.0:
        z = z * out_scale
    if split:
        outs[0][...] = z

        @pl.when(pl.program_id(0) == pl.num_programs(0) - 1)
        def _():
            outs[1][...] = z[row_tile - N_SAMPLE:, :]
        outs = outs[2:]
    for o in outs:
        o[...] = z.astype(o.dtype)


def _norm_matmul(x, g, w, layer, row_tile, col_tile, act, out_dtypes, out_scale=1.0, split=False, forget=None,
                 casts=()):
    m = x.shape[0]
    n = w.shape[2]
    grid = (m // row_tile, n // col_tile)
    assert forget is None or col_tile == n
    w_mode = dict(pipeline_mode=pl.Buffered(1)) if col_tile == n else {}
    out_spec = pl.BlockSpec((row_tile, col_tile), lambda i, j: (i, j))
    out_specs = [out_spec] * len(out_dtypes)
    out_shape = [jax.ShapeDtypeStruct((m, n), d) for d in out_dtypes]
    if split:
        last = grid[0] - 1
        sample_spec = pl.BlockSpec((N_SAMPLE, col_tile), lambda i, j: (0, jnp.where(i == last, j, 0)))
        out_specs = [out_spec, sample_spec] + out_specs
        out_shape = [jax.ShapeDtypeStruct((SEQ, n), F32), jax.ShapeDtypeStruct((N_SAMPLE, n), F32)] + out_shape
    in_specs = [
        pl.BlockSpec((row_tile, D_MODEL), lambda i, j: (i, 0)),
        pl.BlockSpec((1, D_MODEL), lambda i, j: (0, 0)),
        pl.BlockSpec((None, D_MODEL, col_tile), lambda i, j: (layer, 0, j), **w_mode),
    ]
    operands = [x, g, w]
    if forget is not None:
        in_specs += [pl.BlockSpec((None, D_MODEL, LANES), lambda i, j: (0, 0, 0)),
                     pl.BlockSpec((1, LANES), lambda i, j: (0, 0))]
        operands += list(forget)
        out_specs = out_specs + [pl.BlockSpec((row_tile, LANES), lambda i, j: (i, 0))]
        out_shape = out_shape + [jax.ShapeDtypeStruct((m, LANES), F32)]
    cast_slabs = []
    for src, src_layer, axis in casts:
        rows, cols = src.shape[1:]
        n_slabs = (cols if axis == 1 else rows) // LANES
        slab = lambda i, j, n_slabs=n_slabs: jnp.minimum(i * grid[1] + j, n_slabs - 1)
        if axis == 1:
            in_specs.append(pl.BlockSpec((None, rows, LANES), lambda i, j, s=slab, l=src_layer: (l, 0, s(i, j))))
            out_specs = out_specs + [pl.BlockSpec((rows, LANES), lambda i, j, s=slab: (0, s(i, j)))]
        else:
            in_specs.append(pl.BlockSpec((None, LANES, cols), lambda i, j, s=slab, l=src_layer: (l, s(i, j), 0)))
            out_specs = out_specs + [pl.BlockSpec((LANES, cols), lambda i, j, s=slab: (s(i, j), 0))]
        operands.append(src)
        out_shape = out_shape + [jax.ShapeDtypeStruct((rows, cols), BF16)]
        cast_slabs.append(n_slabs)
    return pl.pallas_call(
        functools.partial(_norm_matmul_kernel, act=act, n_out=len(out_shape), out_scale=out_scale,
                          split=split, row_tile=row_tile, forget=forget is not None,
                          cast_slabs=tuple(cast_slabs)),
        grid=grid,
        in_specs=in_specs,
        out_specs=out_specs,
        out_shape=out_shape,
        scratch_shapes=[pltpu.VMEM((row_tile, D_MODEL), BF16)],
        compiler_params=_params("arbitrary" if split or casts else "parallel", "arbitrary"),
        name="norm_matmul_" + str(act),
    )(*operands)


def _matmul_residual_kernel(ap_ref, as_ref, w_ref, x_ref, o_ref):
    i, last_i = pl.program_id(0), pl.num_programs(0) - 1

    def project(a):
        o_ref[...] = x_ref[...] + jnp.dot(a, w_ref[...].astype(BF16), preferred_element_type=F32)

    @pl.when(i < last_i)
    def _():
        project(ap_ref[...])

    @pl.when(i == last_i)
    def _():
        project(jnp.concatenate([ap_ref[:SMALL_ROW_TILE - N_SAMPLE, :], as_ref[...]], axis=0))


def _matmul_residual(a_prompt, a_sample, w, layer, x):
    m = x.shape[0]
    k, n = w.shape[1:]
    return pl.pallas_call(
        _matmul_residual_kernel,
        grid=(m // SMALL_ROW_TILE,),
        in_specs=[
            pl.BlockSpec((SMALL_ROW_TILE, k), lambda i: (i, 0)),
            pl.BlockSpec((N_SAMPLE, k), lambda i: (0, 0)),
            pl.BlockSpec((None, k, n), lambda i: (layer, 0, 0), pipeline_mode=pl.Buffered(1)),
            pl.BlockSpec((SMALL_ROW_TILE, n), lambda i: (i, 0)),
        ],
        out_specs=pl.BlockSpec((SMALL_ROW_TILE, n), lambda i: (i, 0)),
        out_shape=jax.ShapeDtypeStruct((m, n), F32),
        compiler_params=_params("parallel"),
        name="matmul_residual",
    )(a_prompt, a_sample, w, x)


def _gmlp_mix_kernel(u_ref, v_ref, x_ref, lng_ref, lnb_ref, ws_ref, bs_ref, wo_ref,
                     o_ref, vn_ref, gated_ref, *, n_prompt_tiles):
    i = pl.program_id(0)
    v = v_ref[...]
    mu = jnp.mean(v, axis=-1, keepdims=True)
    vc = v - mu
    vn = vc * lax.rsqrt(jnp.mean(vc * vc, axis=-1, keepdims=True) + LN_EPS)
    vn = vn * lng_ref[...] + lnb_ref[...]

    is_sample = i >= n_prompt_tiles

    @pl.when(is_sample)
    def _():
        vn_ref[...] = vn

    sel = is_sample.astype(jnp.int32)
    shift = jnp.where(is_sample, int(math.log2(DEC_SEQ)), int(math.log2(GMLP_CHUNK)))
    t = lax.broadcasted_iota(jnp.int32, (GMLP_CHUNK, GMLP_CHUNK), 0)
    s = lax.broadcasted_iota(jnp.int32, (GMLP_CHUNK, GMLP_CHUNK), 1)
    mask = (s <= t) & ((t >> shift) == (s >> shift))
    vnb = vn.astype(BF16)
    for grp in range(GMLP_GROUPS):
        w = jnp.where(mask, ws_ref[sel, grp], 0.0).astype(BF16)
        bias = bs_ref[sel, grp]
        cols = slice(grp * GROUP_W, (grp + 1) * GROUP_W)
        for c in range(GMLP_ROW_TILE // GMLP_CHUNK):
            rows = slice(c * GMLP_CHUNK, (c + 1) * GMLP_CHUNK)
            mixed = jnp.dot(w, vnb[rows, cols], preferred_element_type=F32) + bias
            gated_ref[rows, cols] = (u_ref[rows, cols] * mixed).astype(BF16)
    o_ref[...] = x_ref[...] + jnp.dot(gated_ref[...], wo_ref[...], preferred_element_type=F32)


def _gmlp_mix(z, x, ln_g, ln_b, ws2, bs2, w_out):
    m = x.shape[0]
    n_prompt_tiles = SEQ // GMLP_ROW_TILE
    single = pl.Buffered(1)
    return pl.pallas_call(
        functools.partial(_gmlp_mix_kernel, n_prompt_tiles=n_prompt_tiles),
        grid=(m // GMLP_ROW_TILE,),
        in_specs=[
            pl.BlockSpec((GMLP_ROW_TILE, D_GATE), lambda i: (i, 0)),
            pl.BlockSpec((GMLP_ROW_TILE, D_GATE), lambda i: (i, 1)),
            pl.BlockSpec((GMLP_ROW_TILE, D_MODEL), lambda i: (i, 0)),
            pl.BlockSpec((1, D_GATE), lambda i: (0, 0)),
            pl.BlockSpec((1, D_GATE), lambda i: (0, 0)),
            pl.BlockSpec((2, GMLP_GROUPS, GMLP_CHUNK, GMLP_CHUNK), lambda i: (0, 0, 0, 0)),
            pl.BlockSpec((2, GMLP_GROUPS, GMLP_CHUNK, 1), lambda i: (0, 0, 0, 0)),
            pl.BlockSpec((D_GATE, D_MODEL), lambda i: (0, 0), pipeline_mode=single),
        ],
        out_specs=[
            pl.BlockSpec((GMLP_ROW_TILE, D_MODEL), lambda i: (i, 0)),
            pl.BlockSpec((GMLP_ROW_TILE, D_GATE),
                         lambda i: (jnp.maximum(i - n_prompt_tiles, 0), 0), pipeline_mode=single),
        ],
        out_shape=[
            jax.ShapeDtypeStruct((m, D_MODEL), F32),
            jax.ShapeDtypeStruct((N_SAMPLE, D_GATE), F32),
        ],
        scratch_shapes=[pltpu.VMEM((GMLP_ROW_TILE, D_GATE), BF16)],
        compiler_params=_params("arbitrary"),
        name="gmlp_mix",
    )(z, z, x, ln_g, ln_b, ws2, bs2, w_out)


def _scan_kernel(x_ref, o_ref):
    x = x_ref[...]
    n = x.shape[-1]
    idx = lax.broadcasted_iota(jnp.int32, x.shape, 1)
    step = 1
    while step < n:
        x = x + jnp.where(idx >= step, pltpu.roll(x, step, axis=1), 0.0)
        step *= 2
    o_ref[...] = x * LOG2E


def _cumsum_lanes(x):
    return pl.pallas_call(
        _scan_kernel,
        out_shape=jax.ShapeDtypeStruct(x.shape, F32),
        compiler_params=pltpu.CompilerParams(vmem_limit_bytes=VMEM_LIMIT),
        name="cumsum_lanes",
    )(x)


def _tile_lanes(x, reps):
    return jnp.concatenate([x] * reps, axis=1)


def _softmax_step(logits, v, m_ref, acc_ref):
    reps = logits.shape[1] // LANES
    m_prev = m_ref[...]
    m_new = jnp.maximum(m_prev, jnp.max(logits, axis=1, keepdims=True))
    p = jnp.exp2(logits - _tile_lanes(m_new, reps))
    alpha = jnp.exp2(m_prev - m_new)
    v_ones = jnp.concatenate([v, jnp.ones_like(v)], axis=1)
    pv = jnp.dot(p.astype(BF16), v_ones, preferred_element_type=F32)
    acc_ref[...] = _tile_lanes(alpha, 2) * acc_ref[...] + pv
    m_ref[...] = m_new


def _softmax_result(acc_ref):
    acc = acc_ref[...]
    return acc[:, :HEAD_DIM] / acc[:, HEAD_DIM:]


def _fox_prompt_kernel(q_ref, k_ref, v_ref, c_ref, cq_rows_ref, o_ref, m_ref, acc_ref, cq_ref):
    qi = pl.program_id(1)
    per_key_tile = ATT_TILE // ATT_Q_TILE
    diag = qi // per_key_tile
    first_row = (qi % per_key_tile) * ATT_Q_TILE
    for hh in range(HEADS_PER_STEP):
        cq_ref[hh] = jnp.broadcast_to(cq_rows_ref[hh, qi], (LANES, ATT_Q_TILE)).T
    m_ref[...] = jnp.full_like(m_ref, NEG_INF)
    acc_ref[...] = jnp.zeros_like(acc_ref)

    def block(kj, on_diagonal):
        start = pl.multiple_of(kj * ATT_TILE, ATT_TILE)
        for hh in range(HEADS_PER_STEP):
            cols = slice(hh * HEAD_DIM, (hh + 1) * HEAD_DIM)
            k = k_ref[pl.ds(start, ATT_TILE), cols]
            s = lax.dot_general(q_ref[:, cols], k, (((1,), (1,)), ((), ())), preferred_element_type=F32)
            logits = s + (_tile_lanes(cq_ref[hh], ATT_TILE // LANES) - c_ref[hh, kj])
            if on_diagonal:
                row = lax.broadcasted_iota(jnp.int32, (ATT_Q_TILE, ATT_TILE), 0) + first_row
                col = lax.broadcasted_iota(jnp.int32, (ATT_Q_TILE, ATT_TILE), 1)
                logits = jnp.where(row >= col, logits, NEG_INF)
            _softmax_step(logits, v_ref[pl.ds(start, ATT_TILE), cols], m_ref.at[hh], acc_ref.at[hh])

    def body(kj, carry):
        block(kj, False)
        return carry

    lax.fori_loop(0, diag, body, 0)
    block(diag, True)
    for hh in range(HEADS_PER_STEP):
        cols = slice(hh * HEAD_DIM, (hh + 1) * HEAD_DIM)
        o_ref[:, cols] = _softmax_result(acc_ref.at[hh]).astype(o_ref.dtype)


def _fox_prompt(q, k, v, c_tiles):
    n_tiles = SEQ // ATT_TILE
    n_q_tiles = SEQ // ATT_Q_TILE
    width = HEADS_PER_STEP * HEAD_DIM
    cq_rows = c_tiles.reshape(N_HEADS, n_q_tiles, 1, ATT_Q_TILE)
    return pl.pallas_call(
        _fox_prompt_kernel,
        grid=(N_HEADS // HEADS_PER_STEP, n_q_tiles),
        in_specs=[
            pl.BlockSpec((ATT_Q_TILE, width), lambda h, i: (i, h)),
            pl.BlockSpec((SEQ, width), lambda h, i: (0, h), pipeline_mode=pl.Buffered(1)),
            pl.BlockSpec((SEQ, width), lambda h, i: (0, h), pipeline_mode=pl.Buffered(1)),
            pl.BlockSpec((HEADS_PER_STEP, n_tiles, 1, ATT_TILE), lambda h, i: (h, 0, 0, 0)),
            pl.BlockSpec((HEADS_PER_STEP, n_q_tiles, 1, ATT_Q_TILE), lambda h, i: (h, 0, 0, 0)),
        ],
        out_specs=pl.BlockSpec((ATT_Q_TILE, width), lambda h, i: (i, h)),
        out_shape=jax.ShapeDtypeStruct((SEQ, D_MODEL), BF16),
        scratch_shapes=[
            pltpu.VMEM((HEADS_PER_STEP, ATT_Q_TILE, LANES), F32),
            pltpu.VMEM((HEADS_PER_STEP, ATT_Q_TILE, 2 * HEAD_DIM), F32),
            pltpu.VMEM((HEADS_PER_STEP, ATT_Q_TILE, LANES), F32),
        ],
        compiler_params=_params("parallel", "arbitrary"),
        name="fox_prompt",
    )(q, k, v, c_tiles, cq_rows)


def _fox_sample_kernel(q_ref, kc_ref, vc_ref, kn_ref, vn_ref, cc_ref, cn_ref, o_ref, m_ref, acc_ref):
    j = pl.program_id(1)

    @pl.when(j == 0)
    def _():
        m_ref[...] = jnp.full_like(m_ref, NEG_INF)
        acc_ref[...] = jnp.zeros_like(acc_ref)

    c_new = cn_ref[...]
    c_pad = jnp.concatenate([c_new, jnp.zeros((LANES - N_HEADS, LANES), F32)], axis=0)
    cq_all = c_pad.T[:DEC_SEQ, :]

    def cq_of(h):
        return jnp.broadcast_to(cq_all[:, h:h + 1], (DEC_SEQ, LANES))

    for h in range(N_HEADS):
        cols = slice(h * HEAD_DIM, (h + 1) * HEAD_DIM)
        head_rows = pl.ds(h, CACHE_TILE, stride=N_HEADS)
        k = kc_ref[head_rows, :].astype(BF16)
        s = lax.dot_general(q_ref[:, cols], k, (((1,), (1,)), ((), ())), preferred_element_type=F32)
        logits = s + (_tile_lanes(cq_of(h), CACHE_TILE // LANES) - cc_ref[h:h + 1, :])
        _softmax_step(logits, vc_ref[head_rows, :].astype(BF16), m_ref.at[h], acc_ref.at[h])

    @pl.when(j == pl.num_programs(1) - 1)
    def _():
        row = lax.broadcasted_iota(jnp.int32, (DEC_SEQ, LANES), 0)
        col = lax.broadcasted_iota(jnp.int32, (DEC_SEQ, LANES), 1)
        pad = jnp.zeros((LANES - DEC_SEQ, HEAD_DIM), BF16)
        for h in range(N_HEADS):
            cols = slice(h * HEAD_DIM, (h + 1) * HEAD_DIM)
            k = jnp.concatenate([kn_ref[:, cols], pad], axis=0)
            v = jnp.concatenate([vn_ref[:, cols], pad], axis=0)
            s = lax.dot_general(q_ref[:, cols], k, (((1,), (1,)), ((), ())), preferred_element_type=F32)
            logits = s + (cq_of(h) - c_new[h:h + 1, :])
            logits = jnp.where(row >= col, logits, NEG_INF)
            _softmax_step(logits, v, m_ref.at[h], acc_ref.at[h])
            o_ref[:, cols] = _softmax_result(acc_ref.at[h]).astype(o_ref.dtype)


def _fox_sample(q, k_new, v_new, cache_k, cache_v, c_sample):
    first = SEQ // DEC_SEQ
    n_steps = PAST_LEN // CACHE_TILE
    new_spec = pl.BlockSpec((DEC_SEQ, D_MODEL), lambda b, j: (first + b, 0))
    cache_spec = pl.BlockSpec((None, CACHE_TILE * N_HEADS, HEAD_DIM), lambda b, j: (b, j, 0))
    cache_k = cache_k.reshape(DEC_BATCH, PAST_LEN * N_HEADS, HEAD_DIM)
    cache_v = cache_v.reshape(DEC_BATCH, PAST_LEN * N_HEADS, HEAD_DIM)
    return pl.pallas_call(
        _fox_sample_kernel,
        grid=(DEC_BATCH, n_steps),
        in_specs=[
            new_spec, cache_spec, cache_spec, new_spec, new_spec,
            pl.BlockSpec((None, N_HEADS, CACHE_TILE), lambda b, j: (b, 0, j)),
            pl.BlockSpec((None, N_HEADS, LANES), lambda b, j: (b, 0, PAST_LEN // LANES)),
        ],
        out_specs=pl.BlockSpec((DEC_SEQ, D_MODEL), lambda b, j: (b, 0)),
        out_shape=jax.ShapeDtypeStruct((N_SAMPLE, D_MODEL), BF16),
        scratch_shapes=[
            pltpu.VMEM((N_HEADS, DEC_SEQ, LANES), F32),
            pltpu.VMEM((N_HEADS, DEC_SEQ, 2 * HEAD_DIM), F32),
        ],
        compiler_params=_params("parallel", "arbitrary"),
        name="fox_sample",
    )(q, cache_k, cache_v, k_new, v_new, c_sample, c_sample)


def kernel(x_prompt, x_sample, cache_k, cache_v, cache_logf, ffn1_norm, ffn1_w_gate, ffn1_w_up,
           ffn1_w_down, mix_norm, ffn2_norm, ffn2_w_gate, ffn2_w_up, ffn2_w_down, gmlp_w_in,
           gmlp_ln_g, gmlp_ln_b, gmlp_w_s, gmlp_b_s, gmlp_w_out, kv_norm, w_k, w_v, w_f, b_f,
           fox_w_q, fox_w_o, final_norm):
    bf = lambda w: w.astype(BF16)
    x = x_prompt.reshape(SEQ, D_MODEL)

    ffn1 = (ffn1_w_gate, ffn1_w_up, ffn1_w_down)
    ffn2 = (ffn2_w_gate, ffn2_w_up, ffn2_w_down)
    final_g = final_norm.reshape(1, D_MODEL)
    w_in_bf, w_out_bf = bf(gmlp_w_in[0])[None], bf(gmlp_w_out[0])

    reps = GMLP_CHUNK // DEC_SEQ
    ws2 = jnp.stack([gmlp_w_s, jnp.tile(gmlp_w_s[:, :, :DEC_SEQ, :DEC_SEQ], (1, 1, reps, reps))], axis=1)
    bs2 = jnp.stack([gmlp_b_s, jnp.tile(gmlp_b_s[:, :, :DEC_SEQ], (1, 1, reps))], axis=1)[..., None]

    gmlp_v = []
    logf_all = None
    for l in range(DEPTH):
        if l == N_A_LAYERS:
            kvg = kv_norm.reshape(1, D_MODEL)
            w_f_pad = jnp.pad(w_f, ((0, 0), (0, LANES - N_HEADS)))[None]
            b_f_pad = jnp.pad(b_f, (0, LANES - N_HEADS)).reshape(1, LANES)
            k_prompt, k_sample, k_bf = _norm_matmul(x, kvg, w_k[None], 0, SMALL_ROW_TILE, D_MODEL, None,
                                                    (BF16,), split=True)
            v_prompt, v_sample, v_bf, logf_pad = _norm_matmul(x, kvg, w_v[None], 0, SMALL_ROW_TILE, D_MODEL, None,
                                                              (BF16,), split=True, forget=(w_f_pad, b_f_pad))
            logf_all = logf_pad[:, :N_HEADS]
            c_prompt = _cumsum_lanes(logf_all[:SEQ].T)
            c_prompt = c_prompt.reshape(N_HEADS, SEQ // ATT_TILE, 1, ATT_TILE)
            lf_new = logf_all[SEQ:].reshape(DEC_BATCH, DEC_SEQ, N_HEADS).transpose(0, 2, 1)
            lf_cache = cache_logf.transpose(0, 2, 1)
            lf = jnp.concatenate(
                [lf_cache, lf_new, jnp.zeros((DEC_BATCH, N_HEADS, LANES - DEC_SEQ), F32)], axis=-1)
            c_sample = _cumsum_lanes(lf.reshape(DEC_BATCH * N_HEADS, PAST_LEN + LANES))
            c_sample = c_sample.reshape(DEC_BATCH, N_HEADS, PAST_LEN + LANES)

        first = (dict(weights=ffn1, layer=0, x_sample=x_sample.reshape(N_SAMPLE, D_MODEL)) if l == 0
                 else dict(weights=ffn_bf))
        x, *ffn_bf = _half_ffn(x, ffn1_norm[l].reshape(1, D_MODEL), next_weights=ffn2, next_layer=l, **first)
        mg = mix_norm[l].reshape(1, D_MODEL)
        if l < N_A_LAYERS:
            later = ((gmlp_w_in, l + 1, 1), (gmlp_w_out, l + 1, 0)) if l + 1 < N_A_LAYERS else ()
            z, *rounded = _norm_matmul(x, mg, w_in_bf, 0, ROW_TILE, 1024, "gelu", (F32,), casts=later)
            x, vn = _gmlp_mix(z, x, gmlp_ln_g[l].reshape(1, D_GATE), gmlp_ln_b[l].reshape(1, D_GATE),
                              ws2[l], bs2[l], w_out_bf)
            if rounded:
                w_in_bf, w_out_bf = rounded[0][None], rounded[1]
            gmlp_v.append(vn.reshape(DEC_BATCH, DEC_SEQ, D_GATE))
        else:
            jj = l - N_A_LAYERS
            (q,) = _norm_matmul(x, mg, fox_w_q, jj, SMALL_ROW_TILE, D_MODEL, None, (BF16,),
                                out_scale=HEAD_DIM ** -0.5 * LOG2E)
            o_prompt = _fox_prompt(q, k_bf, v_bf, c_prompt)
            o_sample = _fox_sample(q, k_bf, v_bf, cache_k, cache_v, c_sample)
            x = _matmul_residual(o_prompt, o_sample, fox_w_o, jj, x)
        g2 = ffn2_norm[l].reshape(1, D_MODEL)
        if l < DEPTH - 1:
            x, *ffn_bf = _half_ffn(x, g2, ffn_bf, ffn1, l + 1)
        else:
            x = _half_ffn(x, g2, ffn_bf, final_gain=final_g)

    y_prompt, y_sample = x
    hd = (N_HEADS, HEAD_DIM)
    return (y_prompt.reshape(1, SEQ, D_MODEL),
            y_sample.reshape(DEC_BATCH, DEC_SEQ, D_MODEL),
            k_prompt.reshape(1, SEQ, *hd),
            v_prompt.reshape(1, SEQ, *hd),
            logf_all[:SEQ].reshape(1, SEQ, N_HEADS),
            k_sample.reshape(DEC_BATCH, DEC_SEQ, *hd),
            v_sample.reshape(DEC_BATCH, DEC_SEQ, *hd),
            logf_all[SEQ:].reshape(DEC_BATCH, DEC_SEQ, N_HEADS),
            jnp.stack(gmlp_v, axis=0))
```

```python
import functools
import math

import jax
import jax.numpy as jnp
from jax import lax
from jax.experimental import pallas as pl
from jax.experimental.pallas import tpu as pltpu

D_MODEL = 2048
SEQ = 8192
DEPTH = 4
DEC_BATCH = 16
DEC_SEQ = 16
PAST_LEN = 2048
N_A_LAYERS = DEPTH // 2
D_FF = 5632
GMLP_CHUNK = 128
D_GATE = 2 * D_MODEL
GMLP_GROUPS = 4
GROUP_W = D_GATE // GMLP_GROUPS
N_HEADS = 16
HEAD_DIM = D_MODEL // N_HEADS
RMS_EPS = 1e-6
LN_EPS = 1e-5
NEG_INF = -1e30
LOG2E = math.log2(math.e)

N_SAMPLE = DEC_BATCH * DEC_SEQ
M_ROWS = SEQ + N_SAMPLE
LANES = 128

F32 = jnp.float32
BF16 = jnp.bfloat16

ROW_TILE = 768
SMALL_ROW_TILE = 384
FFN_ROW_TILE = 1056
FF_TILE = 512
FFN_CAST_TILE = 128
GMLP_ROW_TILE = 256
ATT_TILE = 512
ATT_Q_TILE = 512
HEADS_PER_STEP = 8
CACHE_TILE = 512
VMEM_LIMIT = 58 * 1024 * 1024


def _params(*sem):
    return pltpu.CompilerParams(dimension_semantics=sem, vmem_limit_bytes=VMEM_LIMIT)


def _rms(xf, g):
    y = xf * lax.rsqrt(jnp.mean(xf * xf, axis=-1, keepdims=True) + RMS_EPS)
    return y * g


def _ffn_kernel(*refs, last, f32_weights, split_input):
    x_ref, refs = refs[0], refs[1:]
    if split_input:
        xs_ref, refs = refs[0], refs[1:]
    g_ref, wg_ref, wu_ref, wd_ref = refs[:4]
    h_ref = refs[-1]
    i, j = pl.program_id(0), pl.program_id(1)
    last_i = pl.num_programs(0) - 1
    if last:
        gf_ref, o_ref, sample_ref = refs[4:7]
    else:
        next_refs, o_ref, cast_refs = refs[4:7], refs[7], refs[8:11]

        @pl.when(i * pl.num_programs(1) + j < D_FF // FFN_CAST_TILE)
        def _():
            for src, dst in zip(next_refs, cast_refs):
                dst[...] = src[...].astype(BF16)

    def with_rows(fn):
        if not split_input:
            fn(x_ref[...])
            return

        @pl.when(i < last_i)
        def _():
            fn(x_ref[...])

        @pl.when(i == last_i)
        def _():
            fn(jnp.concatenate([x_ref[:FFN_ROW_TILE - N_SAMPLE, :], xs_ref[...]], axis=0))

    @pl.when(j == 0)
    def _():
        def normalise(rows):
            h_ref[...] = _rms(rows, g_ref[...]).astype(BF16)

        with_rows(normalise)
        o_ref[...] = jnp.zeros_like(o_ref)

    h = h_ref[...]
    wg, wu, wd = wg_ref[...], wu_ref[...], wd_ref[...]
    if f32_weights:
        wg, wu, wd = wg.astype(BF16), wu.astype(BF16), wd.astype(BF16)
    a = jnp.dot(h, wg, preferred_element_type=F32)
    b = jnp.dot(h, wu, preferred_element_type=F32)
    act = (a * jax.nn.sigmoid(a) * b).astype(BF16)
    o_ref[...] += jnp.dot(act, wd, preferred_element_type=F32)

    @pl.when(j == pl.num_programs(1) - 1)
    def _():
        def residual(rows):
            r = rows + 0.5 * o_ref[...]
            if last:
                r = _rms(r, gf_ref[...])
            o_ref[...] = r
            if last:
                @pl.when(i == last_i)
                def _():
                    sample_ref[...] = r[FFN_ROW_TILE - N_SAMPLE:, :]

        with_rows(residual)


def _half_ffn(x, g, weights, next_weights=None, next_layer=None, final_gain=None, layer=None, x_sample=None):
    m = M_ROWS
    f32_weights = layer is not None
    ff_tile = FF_TILE // 2 if f32_weights else FF_TILE
    n_ff = D_FF // ff_tile
    grid = (m // FFN_ROW_TILE, n_ff)
    last = next_weights is None
    row_spec = pl.BlockSpec((FFN_ROW_TILE, D_MODEL), lambda i, j: (i, 0), pipeline_mode=pl.Buffered(1))
    in_specs = [pl.BlockSpec((FFN_ROW_TILE, D_MODEL), lambda i, j: (i, 0))]
    rows = (x,)
    if x_sample is not None:
        in_specs.append(pl.BlockSpec((N_SAMPLE, D_MODEL), lambda i, j: (0, 0)))
        rows = (x, x_sample)
    in_specs.append(pl.BlockSpec((1, D_MODEL), lambda i, j: (0, 0)))
    if f32_weights:
        in_specs += [
            pl.BlockSpec((None, D_MODEL, ff_tile), lambda i, j: (layer, 0, j)),
            pl.BlockSpec((None, D_MODEL, ff_tile), lambda i, j: (layer, 0, j)),
            pl.BlockSpec((None, ff_tile, D_MODEL), lambda i, j: (layer, j, 0)),
        ]
    else:
        in_specs += [
            pl.BlockSpec((D_MODEL, ff_tile), lambda i, j: (0, j)),
            pl.BlockSpec((D_MODEL, ff_tile), lambda i, j: (0, j)),
            pl.BlockSpec((ff_tile, D_MODEL), lambda i, j: (j, 0)),
        ]
    if last:
        in_specs.append(pl.BlockSpec((1, D_MODEL), lambda i, j: (0, 0)))
        operands = (*rows, g, *weights, final_gain)
        out_specs = [row_spec, pl.BlockSpec((N_SAMPLE, D_MODEL), lambda i, j: (0, 0))]
        out_shape = [jax.ShapeDtypeStruct((SEQ, D_MODEL), F32), jax.ShapeDtypeStruct((N_SAMPLE, D_MODEL), F32)]
    else:
        slab = lambda i, j: jnp.minimum(i * n_ff + j, D_FF // FFN_CAST_TILE - 1)
        in_specs += [
            pl.BlockSpec((None, D_MODEL, FFN_CAST_TILE), lambda i, j: (next_layer, 0, slab(i, j))),
            pl.BlockSpec((None, D_MODEL, FFN_CAST_TILE), lambda i, j: (next_layer, 0, slab(i, j))),
            pl.BlockSpec((None, FFN_CAST_TILE, D_MODEL), lambda i, j: (next_layer, slab(i, j), 0)),
        ]
        operands = (*rows, g, *weights, *next_weights)
        out_specs = [
            row_spec,
            pl.BlockSpec((D_MODEL, FFN_CAST_TILE), lambda i, j: (0, slab(i, j))),
            pl.BlockSpec((D_MODEL, FFN_CAST_TILE), lambda i, j: (0, slab(i, j))),
            pl.BlockSpec((FFN_CAST_TILE, D_MODEL), lambda i, j: (slab(i, j), 0)),
        ]
        out_shape = [
            jax.ShapeDtypeStruct((m, D_MODEL), F32),
            jax.ShapeDtypeStruct((D_MODEL, D_FF), BF16),
            jax.ShapeDtypeStruct((D_MODEL, D_FF), BF16),
            jax.ShapeDtypeStruct((D_FF, D_MODEL), BF16),
        ]
    return pl.pallas_call(
        functools.partial(_ffn_kernel, last=last, f32_weights=f32_weights, split_input=x_sample is not None),
        grid=grid,
        in_specs=in_specs,
        out_specs=out_specs,
        out_shape=out_shape,
        scratch_shapes=[pltpu.VMEM((FFN_ROW_TILE, D_MODEL), BF16)],
        compiler_params=_params("arbitrary", "arbitrary"),
        name="half_ffn",
    )(*operands)


def _gelu(z):
    return 0.5 * z * (1.0 + lax.erf(z * (1.0 / math.sqrt(2.0))))


def _log_sigmoid(z):
    return -(jnp.maximum(-z, 0.0) + jnp.log1p(jnp.exp(-jnp.abs(z))))


def _norm_matmul_kernel(x_ref, g_ref, w_ref, *rest, act, n_out, out_scale, split, row_tile, forget, cast_slabs):
    if forget:
        wf_ref, bf_ref, rest = rest[0], rest[1], rest[2:]
    n_cast = len(cast_slabs)
    cast_src, rest = rest[:n_cast], rest[n_cast:]
    outs, h_ref = rest[:n_out], rest[n_out]
    outs, cast_dst = outs[:n_out - n_cast], outs[n_out - n_cast:]
    j = pl.program_id(1)
    for src, dst, n_slabs in zip(cast_src, cast_dst, cast_slabs):
        @pl.when(pl.program_id(0) * pl.num_programs(1) + j < n_slabs)
        def _(src=src, dst=dst):
            dst[...] = src[...].astype(BF16)

    @pl.when(j == 0)
    def _():
        h_ref[...] = _rms(x_ref[...], g_ref[...]).astype(BF16)

    z = jnp.dot(h_ref[...], w_ref[...].astype(BF16), preferred_element_type=F32)
    if act == "gelu":
        z = _gelu(z)
    if forget:
        zf = jnp.dot(h_ref[...], wf_ref[...].astype(BF16), preferred_element_type=F32)
        outs[-1][...] = _log_sigmoid(zf + bf_ref[...])
        outs = outs[:-1]
    if out_scale != 1.0:
        z = z * out_scale
    if split:
        outs[0][...] = z

        @pl.when(pl.program_id(0) == pl.num_programs(0) - 1)
        def _():
            outs[1][...] = z[row_tile - N_SAMPLE:, :]
        outs = outs[2:]
    for o in outs:
        o[...] = z.astype(o.dtype)


def _norm_matmul(x, g, w, layer, row_tile, col_tile, act, out_dtypes, out_scale=1.0, split=False, forget=None,
                 casts=()):
    m = x.shape[0]
    n = w.shape[2]
    grid = (m // row_tile, n // col_tile)
    assert forget is None or col_tile == n
    w_mode = dict(pipeline_mode=pl.Buffered(1)) if col_tile == n else {}
    out_spec = pl.BlockSpec((row_tile, col_tile), lambda i, j: (i, j))
    out_specs = [out_spec] * len(out_dtypes)
    out_shape = [jax.ShapeDtypeStruct((m, n), d) for d in out_dtypes]
    if split:
        last = grid[0] - 1
        sample_spec = pl.BlockSpec((N_SAMPLE, col_tile), lambda i, j: (0, jnp.where(i == last, j, 0)))
        out_specs = [out_spec, sample_spec] + out_specs
        out_shape = [jax.ShapeDtypeStruct((SEQ, n), F32), jax.ShapeDtypeStruct((N_SAMPLE, n), F32)] + out_shape
    in_specs = [
        pl.BlockSpec((row_tile, D_MODEL), lambda i, j: (i, 0)),
        pl.BlockSpec((1, D_MODEL), lambda i, j: (0, 0)),
        pl.BlockSpec((None, D_MODEL, col_tile), lambda i, j: (layer, 0, j), **w_mode),
    ]
    operands = [x, g, w]
    if forget is not None:
        in_specs += [pl.BlockSpec((None, D_MODEL, LANES), lambda i, j: (0, 0, 0)),
                     pl.BlockSpec((1, LANES), lambda i, j: (0, 0))]
        operands += list(forget)
        out_specs = out_specs + [pl.BlockSpec((row_tile, LANES), lambda i, j: (i, 0))]
        out_shape = out_shape + [jax.ShapeDtypeStruct((m, LANES), F32)]
    cast_slabs = []
    for src, src_layer, axis in casts:
        rows, cols = src.shape[1:]
        n_slabs = (cols if axis == 1 else rows) // LANES
        slab = lambda i, j, n_slabs=n_slabs: jnp.minimum(i * grid[1] + j, n_slabs - 1)
        if axis == 1:
            in_specs.append(pl.BlockSpec((None, rows, LANES), lambda i, j, s=slab, l=src_layer: (l, 0, s(i, j))))
            out_specs = out_specs + [pl.BlockSpec((rows, LANES), lambda i, j, s=slab: (0, s(i, j)))]
        else:
            in_specs.append(pl.BlockSpec((None, LANES, cols), lambda i, j, s=slab, l=src_layer: (l, s(i, j), 0)))
            out_specs = out_specs + [pl.BlockSpec((LANES, cols), lambda i, j, s=slab: (s(i, j), 0))]
        operands.append(src)
        out_shape = out_shape + [jax.ShapeDtypeStruct((rows, cols), BF16)]
        cast_slabs.append(n_slabs)
    return pl.pallas_call(
        functools.partial(_norm_matmul_kernel, act=act, n_out=len(out_shape), out_scale=out_scale,
                          split=split, row_tile=row_tile, forget=forget is not None,
                          cast_slabs=tuple(cast_slabs)),
        grid=grid,
        in_specs=in_specs,
        out_specs=out_specs,
        out_shape=out_shape,
        scratch_shapes=[pltpu.VMEM((row_tile, D_MODEL), BF16)],
        compiler_params=_params("arbitrary" if split or casts else "parallel", "arbitrary"),
        name="norm_matmul_" + str(act),
    )(*operands)


def _matmul_residual_kernel(ap_ref, as_ref, w_ref, x_ref, o_ref):
    i, last_i = pl.program_id(0), pl.num_programs(0) - 1

    def project(a):
        o_ref[...] = x_ref[...] + jnp.dot(a, w_ref[...].astype(BF16), preferred_element_type=F32)

    @pl.when(i < last_i)
    def _():
        project(ap_ref[...])

    @pl.when(i == last_i)
    def _():
        project(jnp.concatenate([ap_ref[:SMALL_ROW_TILE - N_SAMPLE, :], as_ref[...]], axis=0))


def _matmul_residual(a_prompt, a_sample, w, layer, x):
    m = x.shape[0]
    k, n = w.shape[1:]
    return pl.pallas_call(
        _matmul_residual_kernel,
        grid=(m // SMALL_ROW_TILE,),
        in_specs=[
            pl.BlockSpec((SMALL_ROW_TILE, k), lambda i: (i, 0)),
            pl.BlockSpec((N_SAMPLE, k), lambda i: (0, 0)),
            pl.BlockSpec((None, k, n), lambda i: (layer, 0, 0), pipeline_mode=pl.Buffered(1)),
            pl.BlockSpec((SMALL_ROW_TILE, n), lambda i: (i, 0)),
        ],
        out_specs=pl.BlockSpec((SMALL_ROW_TILE, n), lambda i: (i, 0)),
        out_shape=jax.ShapeDtypeStruct((m, n), F32),
        compiler_params=_params("parallel"),
        name="matmul_residual",
    )(a_prompt, a_sample, w, x)


def _gmlp_mix_kernel(u_ref, v_ref, x_ref, lng_ref, lnb_ref, ws_ref, bs_ref, wo_ref,
                     o_ref, vn_ref, gated_ref, *, n_prompt_tiles):
    i = pl.program_id(0)
    v = v_ref[...]
    mu = jnp.mean(v, axis=-1, keepdims=True)
    vc = v - mu
    vn = vc * lax.rsqrt(jnp.mean(vc * vc, axis=-1, keepdims=True) + LN_EPS)
    vn = vn * lng_ref[...] + lnb_ref[...]

    is_sample = i >= n_prompt_tiles

    @pl.when(is_sample)
    def _():
        vn_ref[...] = vn

    sel = is_sample.astype(jnp.int32)
    shift = jnp.where(is_sample, int(math.log2(DEC_SEQ)), int(math.log2(GMLP_CHUNK)))
    t = lax.broadcasted_iota(jnp.int32, (GMLP_CHUNK, GMLP_CHUNK), 0)
    s = lax.broadcasted_iota(jnp.int32, (GMLP_CHUNK, GMLP_CHUNK), 1)
    mask = (s <= t) & ((t >> shift) == (s >> shift))
    vnb = vn.astype(BF16)
    for grp in range(GMLP_GROUPS):
        w = jnp.where(mask, ws_ref[sel, grp], 0.0).astype(BF16)
        bias = bs_ref[sel, grp]
        cols = slice(grp * GROUP_W, (grp + 1) * GROUP_W)
        for c in range(GMLP_ROW_TILE // GMLP_CHUNK):
            rows = slice(c * GMLP_CHUNK, (c + 1) * GMLP_CHUNK)
            mixed = jnp.dot(w, vnb[rows, cols], preferred_element_type=F32) + bias
            gated_ref[rows, cols] = (u_ref[rows, cols] * mixed).astype(BF16)
    o_ref[...] = x_ref[...] + jnp.dot(gated_ref[...], wo_ref[...], preferred_element_type=F32)


def _gmlp_mix(z, x, ln_g, ln_b, ws2, bs2, w_out):
    m = x.shape[0]
    n_prompt_tiles = SEQ // GMLP_ROW_TILE
    single = pl.Buffered(1)
    return pl.pallas_call(
        functools.partial(_gmlp_mix_kernel, n_prompt_tiles=n_prompt_tiles),
        grid=(m // GMLP_ROW_TILE,),
        in_specs=[
            pl.BlockSpec((GMLP_ROW_TILE, D_GATE), lambda i: (i, 0)),
            pl.BlockSpec((GMLP_ROW_TILE, D_GATE), lambda i: (i, 1)),
            pl.BlockSpec((GMLP_ROW_TILE, D_MODEL), lambda i: (i, 0)),
            pl.BlockSpec((1, D_GATE), lambda i: (0, 0)),
            pl.BlockSpec((1, D_GATE), lambda i: (0, 0)),
            pl.BlockSpec((2, GMLP_GROUPS, GMLP_CHUNK, GMLP_CHUNK), lambda i: (0, 0, 0, 0)),
            pl.BlockSpec((2, GMLP_GROUPS, GMLP_CHUNK, 1), lambda i: (0, 0, 0, 0)),
            pl.BlockSpec((D_GATE, D_MODEL), lambda i: (0, 0), pipeline_mode=single),
        ],
        out_specs=[
            pl.BlockSpec((GMLP_ROW_TILE, D_MODEL), lambda i: (i, 0)),
            pl.BlockSpec((GMLP_ROW_TILE, D_GATE),
                         lambda i: (jnp.maximum(i - n_prompt_tiles, 0), 0), pipeline_mode=single),
        ],
        out_shape=[
            jax.ShapeDtypeStruct((m, D_MODEL), F32),
            jax.ShapeDtypeStruct((N_SAMPLE, D_GATE), F32),
        ],
        scratch_shapes=[pltpu.VMEM((GMLP_ROW_TILE, D_GATE), BF16)],
        compiler_params=_params("arbitrary"),
        name="gmlp_mix",
    )(z, z, x, ln_g, ln_b, ws2, bs2, w_out)


def _scan_kernel(x_ref, o_ref):
    x = x_ref[...]
    n = x.shape[-1]
    idx = lax.broadcasted_iota(jnp.int32, x.shape, 1)
    step = 1
    while step < n:
        x = x + jnp.where(idx >= step, pltpu.roll(x, step, axis=1), 0.0)
        step *= 2
    o_ref[...] = x * LOG2E


def _cumsum_lanes(x):
    return pl.pallas_call(
        _scan_kernel,
        out_shape=jax.ShapeDtypeStruct(x.shape, F32),
        compiler_params=pltpu.CompilerParams(vmem_limit_bytes=VMEM_LIMIT),
        name="cumsum_lanes",
    )(x)


def _tile_lanes(x, reps):
    return jnp.concatenate([x] * reps, axis=1)


def _softmax_step(logits, v, m_ref, acc_ref):
    reps = logits.shape[1] // LANES
    m_prev = m_ref[...]
    m_new = jnp.maximum(m_prev, jnp.max(logits, axis=1, keepdims=True))
    p = jnp.exp2(logits - _tile_lanes(m_new, reps))
    alpha = jnp.exp2(m_prev - m_new)
    v_ones = jnp.concatenate([v, jnp.ones_like(v)], axis=1)
    pv = jnp.dot(p.astype(BF16), v_ones, preferred_element_type=F32)
    acc_ref[...] = _tile_lanes(alpha, 2) * acc_ref[...] + pv
    m_ref[...] = m_new


def _softmax_result(acc_ref):
    acc = acc_ref[...]
    return acc[:, :HEAD_DIM] / acc[:, HEAD_DIM:]


def _fox_prompt_kernel(q_ref, k_ref, v_ref, c_ref, cq_rows_ref, o_ref, m_ref, acc_ref, cq_ref):
    qi = pl.program_id(1)
    per_key_tile = ATT_TILE // ATT_Q_TILE
    diag = qi // per_key_tile
    first_row = (qi % per_key_tile) * ATT_Q_TILE
    for hh in range(HEADS_PER_STEP):
        cq_ref[hh] = jnp.broadcast_to(cq_rows_ref[hh, qi], (LANES, ATT_Q_TILE)).T
    m_ref[...] = jnp.full_like(m_ref, NEG_INF)
    acc_ref[...] = jnp.zeros_like(acc_ref)

    def block(kj, on_diagonal):
        start = pl.multiple_of(kj * ATT_TILE, ATT_TILE)
        for hh in range(HEADS_PER_STEP):
            cols = slice(hh * HEAD_DIM, (hh + 1) * HEAD_DIM)
            k = k_ref[pl.ds(start, ATT_TILE), cols]
            s = lax.dot_general(q_ref[:, cols], k, (((1,), (1,)), ((), ())), preferred_element_type=F32)
            logits = s + (_tile_lanes(cq_ref[hh], ATT_TILE // LANES) - c_ref[hh, kj])
            if on_diagonal:
                row = lax.broadcasted_iota(jnp.int32, (ATT_Q_TILE, ATT_TILE), 0) + first_row
                col = lax.broadcasted_iota(jnp.int32, (ATT_Q_TILE, ATT_TILE), 1)
                logits = jnp.where(row >= col, logits, NEG_INF)
            _softmax_step(logits, v_ref[pl.ds(start, ATT_TILE), cols], m_ref.at[hh], acc_ref.at[hh])

    def body(kj, carry):
        block(kj, False)
        return carry

    lax.fori_loop(0, diag, body, 0)
    block(diag, True)
    for hh in range(HEADS_PER_STEP):
        cols = slice(hh * HEAD_DIM, (hh + 1) * HEAD_DIM)
        o_ref[:, cols] = _softmax_result(acc_ref.at[hh]).astype(o_ref.dtype)


def _fox_prompt(q, k, v, c_tiles):
    n_tiles = SEQ // ATT_TILE
    n_q_tiles = SEQ // ATT_Q_TILE
    width = HEADS_PER_STEP * HEAD_DIM
    cq_rows = c_tiles.reshape(N_HEADS, n_q_tiles, 1, ATT_Q_TILE)
    return pl.pallas_call(
        _fox_prompt_kernel,
        grid=(N_HEADS // HEADS_PER_STEP, n_q_tiles),
        in_specs=[
            pl.BlockSpec((ATT_Q_TILE, width), lambda h, i: (i, h)),
            pl.BlockSpec((SEQ, width), lambda h, i: (0, h), pipeline_mode=pl.Buffered(1)),
            pl.BlockSpec((SEQ, width), lambda h, i: (0, h), pipeline_mode=pl.Buffered(1)),
            pl.BlockSpec((HEADS_PER_STEP, n_tiles, 1, ATT_TILE), lambda h, i: (h, 0, 0, 0)),
            pl.BlockSpec((HEADS_PER_STEP, n_q_tiles, 1, ATT_Q_TILE), lambda h, i: (h, 0, 0, 0)),
        ],
        out_specs=pl.BlockSpec((ATT_Q_TILE, width), lambda h, i: (i, h)),
        out_shape=jax.ShapeDtypeStruct((SEQ, D_MODEL), BF16),
        scratch_shapes=[
            pltpu.VMEM((HEADS_PER_STEP, ATT_Q_TILE, LANES), F32),
            pltpu.VMEM((HEADS_PER_STEP, ATT_Q_TILE, 2 * HEAD_DIM), F32),
            pltpu.VMEM((HEADS_PER_STEP, ATT_Q_TILE, LANES), F32),
        ],
        compiler_params=_params("parallel", "arbitrary"),
        name="fox_prompt",
    )(q, k, v, c_tiles, cq_rows)


def _fox_sample_kernel(q_ref, kc_ref, vc_ref, kn_ref, vn_ref, cc_ref, cn_ref, o_ref, *rest, repack):
    m_ref, acc_ref = rest[-2:]
    j = pl.program_id(1)

    @pl.when(j == 0)
    def _():
        m_ref[...] = jnp.full_like(m_ref, NEG_INF)
        acc_ref[...] = jnp.zeros_like(acc_ref)

    c_new = cn_ref[...]
    c_pad = jnp.concatenate([c_new, jnp.zeros((LANES - N_HEADS, LANES), F32)], axis=0)
    cq_all = c_pad.T[:DEC_SEQ, :]

    def cq_of(h):
        return jnp.broadcast_to(cq_all[:, h:h + 1], (DEC_SEQ, LANES))

    for h in range(N_HEADS):
        cols = slice(h * HEAD_DIM, (h + 1) * HEAD_DIM)
        if repack:
            head_rows = pl.ds(h, CACHE_TILE, stride=N_HEADS)
            k = kc_ref[head_rows, :].astype(BF16)
            v = vc_ref[head_rows, :].astype(BF16)
            rest[0][h] = k
            rest[1][h] = v
        else:
            k, v = kc_ref[h], vc_ref[h]
        s = lax.dot_general(q_ref[:, cols], k, (((1,), (1,)), ((), ())), preferred_element_type=F32)
        logits = s + (_tile_lanes(cq_of(h), CACHE_TILE // LANES) - cc_ref[h:h + 1, :])
        _softmax_step(logits, v, m_ref.at[h], acc_ref.at[h])

    @pl.when(j == pl.num_programs(1) - 1)
    def _():
        row = lax.broadcasted_iota(jnp.int32, (DEC_SEQ, LANES), 0)
        col = lax.broadcasted_iota(jnp.int32, (DEC_SEQ, LANES), 1)
        pad = jnp.zeros((LANES - DEC_SEQ, HEAD_DIM), BF16)
        for h in range(N_HEADS):
            cols = slice(h * HEAD_DIM, (h + 1) * HEAD_DIM)
            k = jnp.concatenate([kn_ref[:, cols], pad], axis=0)
            v = jnp.concatenate([vn_ref[:, cols], pad], axis=0)
            s = lax.dot_general(q_ref[:, cols], k, (((1,), (1,)), ((), ())), preferred_element_type=F32)
            logits = s + (cq_of(h) - c_new[h:h + 1, :])
            logits = jnp.where(row >= col, logits, NEG_INF)
            _softmax_step(logits, v, m_ref.at[h], acc_ref.at[h])
            o_ref[:, cols] = _softmax_result(acc_ref.at[h]).astype(o_ref.dtype)


def _fox_sample(q, k_new, v_new, cache_k, cache_v, c_sample, repack):
    first = SEQ // DEC_SEQ
    n_steps = PAST_LEN // CACHE_TILE
    new_spec = pl.BlockSpec((DEC_SEQ, D_MODEL), lambda b, j: (first + b, 0))
    head_spec = pl.BlockSpec((None, N_HEADS, CACHE_TILE, HEAD_DIM), lambda b, j: (b, 0, j, 0))
    out_specs = pl.BlockSpec((DEC_SEQ, D_MODEL), lambda b, j: (b, 0))
    out_shape = jax.ShapeDtypeStruct((N_SAMPLE, D_MODEL), BF16)
    if repack:
        cache_spec = pl.BlockSpec((None, CACHE_TILE * N_HEADS, HEAD_DIM), lambda b, j: (b, j, 0))
        cache_k = cache_k.reshape(DEC_BATCH, PAST_LEN * N_HEADS, HEAD_DIM)
        cache_v = cache_v.reshape(DEC_BATCH, PAST_LEN * N_HEADS, HEAD_DIM)
        per_head = jax.ShapeDtypeStruct((DEC_BATCH, N_HEADS, PAST_LEN, HEAD_DIM), BF16)
        out_specs, out_shape = [out_specs, head_spec, head_spec], [out_shape, per_head, per_head]
    else:
        cache_spec = head_spec
    return pl.pallas_call(
        functools.partial(_fox_sample_kernel, repack=repack),
        grid=(DEC_BATCH, n_steps),
        in_specs=[
            new_spec, cache_spec, cache_spec, new_spec, new_spec,
            pl.BlockSpec((None, N_HEADS, CACHE_TILE), lambda b, j: (b, 0, j)),
            pl.BlockSpec((None, N_HEADS, LANES), lambda b, j: (b, 0, PAST_LEN // LANES)),
        ],
        out_specs=out_specs,
        out_shape=out_shape,
        scratch_shapes=[
            pltpu.VMEM((N_HEADS, DEC_SEQ, LANES), F32),
            pltpu.VMEM((N_HEADS, DEC_SEQ, 2 * HEAD_DIM), F32),
        ],
        compiler_params=_params("parallel", "arbitrary"),
        name="fox_sample",
    )(q, cache_k, cache_v, k_new, v_new, c_sample, c_sample)


def kernel(x_prompt, x_sample, cache_k, cache_v, cache_logf, ffn1_norm, ffn1_w_gate, ffn1_w_up,
           ffn1_w_down, mix_norm, ffn2_norm, ffn2_w_gate, ffn2_w_up, ffn2_w_down, gmlp_w_in,
           gmlp_ln_g, gmlp_ln_b, gmlp_w_s, gmlp_b_s, gmlp_w_out, kv_norm, w_k, w_v, w_f, b_f,
           fox_w_q, fox_w_o, final_norm):
    bf = lambda w: w.astype(BF16)
    x = x_prompt.reshape(SEQ, D_MODEL)

    ffn1 = (ffn1_w_gate, ffn1_w_up, ffn1_w_down)
    ffn2 = (ffn2_w_gate, ffn2_w_up, ffn2_w_down)
    final_g = final_norm.reshape(1, D_MODEL)
    w_in_bf, w_out_bf = bf(gmlp_w_in[0])[None], bf(gmlp_w_out[0])

    reps = GMLP_CHUNK // DEC_SEQ
    ws2 = jnp.stack([gmlp_w_s, jnp.tile(gmlp_w_s[:, :, :DEC_SEQ, :DEC_SEQ], (1, 1, reps, reps))], axis=1)
    bs2 = jnp.stack([gmlp_b_s, jnp.tile(gmlp_b_s[:, :, :DEC_SEQ], (1, 1, reps))], axis=1)[..., None]

    gmlp_v = []
    logf_all = None
    for l in range(DEPTH):
        if l == N_A_LAYERS:
            kvg = kv_norm.reshape(1, D_MODEL)
            w_f_pad = jnp.pad(w_f, ((0, 0), (0, LANES - N_HEADS)))[None]
            b_f_pad = jnp.pad(b_f, (0, LANES - N_HEADS)).reshape(1, LANES)
            k_prompt, k_sample, k_bf = _norm_matmul(x, kvg, w_k[None], 0, SMALL_ROW_TILE, D_MODEL, None,
                                                    (BF16,), split=True)
            v_prompt, v_sample, v_bf, logf_pad = _norm_matmul(x, kvg, w_v[None], 0, SMALL_ROW_TILE, D_MODEL, None,
                                                              (BF16,), split=True, forget=(w_f_pad, b_f_pad))
            logf_all = logf_pad[:, :N_HEADS]
            c_prompt = _cumsum_lanes(logf_all[:SEQ].T)
            c_prompt = c_prompt.reshape(N_HEADS, SEQ // ATT_TILE, 1, ATT_TILE)
            lf_new = logf_all[SEQ:].reshape(DEC_BATCH, DEC_SEQ, N_HEADS).transpose(0, 2, 1)
            lf_cache = cache_logf.transpose(0, 2, 1)
            lf = jnp.concatenate(
                [lf_cache, lf_new, jnp.zeros((DEC_BATCH, N_HEADS, LANES - DEC_SEQ), F32)], axis=-1)
            c_sample = _cumsum_lanes(lf.reshape(DEC_BATCH * N_HEADS, PAST_LEN + LANES))
            c_sample = c_sample.reshape(DEC_BATCH, N_HEADS, PAST_LEN + LANES)

        first = (dict(weights=ffn1, layer=0, x_sample=x_sample.reshape(N_SAMPLE, D_MODEL)) if l == 0
                 else dict(weights=ffn_bf))
        x, *ffn_bf = _half_ffn(x, ffn1_norm[l].reshape(1, D_MODEL), next_weights=ffn2, next_layer=l, **first)
        mg = mix_norm[l].reshape(1, D_MODEL)
        if l < N_A_LAYERS:
            later = ((gmlp_w_in, l + 1, 1), (gmlp_w_out, l + 1, 0)) if l + 1 < N_A_LAYERS else ()
            z, *rounded = _norm_matmul(x, mg, w_in_bf, 0, ROW_TILE, 1024, "gelu", (F32,), casts=later)
            x, vn = _gmlp_mix(z, x, gmlp_ln_g[l].reshape(1, D_GATE), gmlp_ln_b[l].reshape(1, D_GATE),
                              ws2[l], bs2[l], w_out_bf)
            if rounded:
                w_in_bf, w_out_bf = rounded[0][None], rounded[1]
            gmlp_v.append(vn.reshape(DEC_BATCH, DEC_SEQ, D_GATE))
        else:
            jj = l - N_A_LAYERS
            (q,) = _norm_matmul(x, mg, fox_w_q, jj, SMALL_ROW_TILE, D_MODEL, None, (BF16,),
                                out_scale=HEAD_DIM ** -0.5 * LOG2E)
            o_prompt = _fox_prompt(q, k_bf, v_bf, c_prompt)
            if jj == 0:
                o_sample, cache_k_heads, cache_v_heads = _fox_sample(q, k_bf, v_bf, cache_k, cache_v, c_sample, True)
            else:
                o_sample = _fox_sample(q, k_bf, v_bf, cache_k_heads, cache_v_heads, c_sample, False)
            x = _matmul_residual(o_prompt, o_sample, fox_w_o, jj, x)
        g2 = ffn2_norm[l].reshape(1, D_MODEL)
        if l < DEPTH - 1:
            x, *ffn_bf = _half_ffn(x, g2, ffn_bf, ffn1, l + 1)
        else:
            x = _half_ffn(x, g2, ffn_bf, final_gain=final_g)

    y_prompt, y_sample = x
    hd = (N_HEADS, HEAD_DIM)
    return (y_prompt.reshape(1, SEQ, D_MODEL),
            y_sample.reshape(DEC_BATCH, DEC_SEQ, D_MODEL),
            k_prompt.reshape(1, SEQ, *hd),
            v_prompt.reshape(1, SEQ, *hd),
            logf_all[:SEQ].reshape(1, SEQ, N_HEADS),
            k_sample.reshape(DEC_BATCH, DEC_SEQ, *hd),
            v_sample.reshape(DEC_BATCH, DEC_SEQ, *hd),
            logf_all[SEQ:].reshape(DEC_BATCH, DEC_SEQ, N_HEADS),
            jnp.stack(gmlp_v, axis=0))
```

```python
import functools
import math

import jax
import jax.numpy as jnp
from jax import lax
from jax.experimental import pallas as pl
from jax.experimental.pallas import tpu as pltpu

D_MODEL = 2048
SEQ = 8192
DEPTH = 4
DEC_BATCH = 16
DEC_SEQ = 16
PAST_LEN = 2048
N_A_LAYERS = DEPTH // 2
D_FF = 5632
GMLP_CHUNK = 128
D_GATE = 2 * D_MODEL
GMLP_GROUPS = 4
GROUP_W = D_GATE // GMLP_GROUPS
N_HEADS = 16
HEAD_DIM = D_MODEL // N_HEADS
RMS_EPS = 1e-6
LN_EPS = 1e-5
NEG_INF = -1e30
LOG2E = math.log2(math.e)

N_SAMPLE = DEC_BATCH * DEC_SEQ
M_ROWS = SEQ + N_SAMPLE
LANES = 128

F32 = jnp.float32
BF16 = jnp.bfloat16

ROW_TILE = 768
SMALL_ROW_TILE = 384
FFN_ROW_TILE = 1056
FF_TILE = 512
FFN_CAST_TILE = 128
GMLP_ROW_TILE = 256
ATT_TILE = 512
ATT_Q_TILE = 512
HEADS_PER_STEP = 8
CACHE_TILE = 512
VMEM_LIMIT = 58 * 1024 * 1024


def _params(*sem):
    return pltpu.CompilerParams(dimension_semantics=sem, vmem_limit_bytes=VMEM_LIMIT)


def _rms(xf, g):
    y = xf * lax.rsqrt(jnp.mean(xf * xf, axis=-1, keepdims=True) + RMS_EPS)
    return y * g


def _ffn_kernel(*refs, last, f32_weights, split_input):
    x_ref, refs = refs[0], refs[1:]
    if split_input:
        xs_ref, refs = refs[0], refs[1:]
    g_ref, wg_ref, wu_ref, wd_ref = refs[:4]
    h_ref = refs[-1]
    i, j = pl.program_id(0), pl.program_id(1)
    last_i = pl.num_programs(0) - 1
    if last:
        gf_ref, o_ref, sample_ref = refs[4:7]
    else:
        next_refs, o_ref, cast_refs = refs[4:7], refs[7], refs[8:11]

        @pl.when(i * pl.num_programs(1) + j < D_FF // FFN_CAST_TILE)
        def _():
            for src, dst in zip(next_refs, cast_refs):
                dst[...] = src[...].astype(BF16)

    def with_rows(fn):
        if not split_input:
            fn(x_ref[...])
            return

        @pl.when(i < last_i)
        def _():
            fn(x_ref[...])

        @pl.when(i == last_i)
        def _():
            fn(jnp.concatenate([x_ref[:FFN_ROW_TILE - N_SAMPLE, :], xs_ref[...]], axis=0))

    @pl.when(j == 0)
    def _():
        def normalise(rows):
            h_ref[...] = _rms(rows, g_ref[...]).astype(BF16)

        with_rows(normalise)
        o_ref[...] = jnp.zeros_like(o_ref)

    h = h_ref[...]
    wg, wu, wd = wg_ref[...], wu_ref[...], wd_ref[...]
    if f32_weights:
        wg, wu, wd = wg.astype(BF16), wu.astype(BF16), wd.astype(BF16)
    a = jnp.dot(h, wg, preferred_element_type=F32)
    b = jnp.dot(h, wu, preferred_element_type=F32)
    act = (a * jax.nn.sigmoid(a) * b).astype(BF16)
    o_ref[...] += jnp.dot(act, wd, preferred_element_type=F32)

    @pl.when(j == pl.num_programs(1) - 1)
    def _():
        def residual(rows):
            r = rows + 0.5 * o_ref[...]
            if last:
                r = _rms(r, gf_ref[...])
            o_ref[...] = r
            if last:
                @pl.when(i == last_i)
                def _():
                    sample_ref[...] = r[FFN_ROW_TILE - N_SAMPLE:, :]

        with_rows(residual)


def _half_ffn(x, g, weights, next_weights=None, next_layer=None, final_gain=None, layer=None, x_sample=None):
    m = M_ROWS
    f32_weights = layer is not None
    ff_tile = FF_TILE // 2 if f32_weights else FF_TILE
    n_ff = D_FF // ff_tile
    grid = (m // FFN_ROW_TILE, n_ff)
    last = next_weights is None
    row_spec = pl.BlockSpec((FFN_ROW_TILE, D_MODEL), lambda i, j: (i, 0), pipeline_mode=pl.Buffered(1))
    in_specs = [pl.BlockSpec((FFN_ROW_TILE, D_MODEL), lambda i, j: (i, 0))]
    rows = (x,)
    if x_sample is not None:
        in_specs.append(pl.BlockSpec((N_SAMPLE, D_MODEL), lambda i, j: (0, 0)))
        rows = (x, x_sample)
    in_specs.append(pl.BlockSpec((1, D_MODEL), lambda i, j: (0, 0)))
    if f32_weights:
        in_specs += [
            pl.BlockSpec((None, D_MODEL, ff_tile), lambda i, j: (layer, 0, j)),
            pl.BlockSpec((None, D_MODEL, ff_tile), lambda i, j: (layer, 0, j)),
            pl.BlockSpec((None, ff_tile, D_MODEL), lambda i, j: (layer, j, 0)),
        ]
    else:
        in_specs += [
            pl.BlockSpec((D_MODEL, ff_tile), lambda i, j: (0, j)),
            pl.BlockSpec((D_MODEL, ff_tile), lambda i, j: (0, j)),
            pl.BlockSpec((ff_tile, D_MODEL), lambda i, j: (j, 0)),
        ]
    if last:
        in_specs.append(pl.BlockSpec((1, D_MODEL), lambda i, j: (0, 0)))
        operands = (*rows, g, *weights, final_gain)
        out_specs = [row_spec, pl.BlockSpec((N_SAMPLE, D_MODEL), lambda i, j: (0, 0))]
        out_shape = [jax.ShapeDtypeStruct((SEQ, D_MODEL), F32), jax.ShapeDtypeStruct((N_SAMPLE, D_MODEL), F32)]
    else:
        slab = lambda i, j: jnp.minimum(i * n_ff + j, D_FF // FFN_CAST_TILE - 1)
        in_specs += [
            pl.BlockSpec((None, D_MODEL, FFN_CAST_TILE), lambda i, j: (next_layer, 0, slab(i, j))),
            pl.BlockSpec((None, D_MODEL, FFN_CAST_TILE), lambda i, j: (next_layer, 0, slab(i, j))),
            pl.BlockSpec((None, FFN_CAST_TILE, D_MODEL), lambda i, j: (next_layer, slab(i, j), 0)),
        ]
        operands = (*rows, g, *weights, *next_weights)
        out_specs = [
            row_spec,
            pl.BlockSpec((D_MODEL, FFN_CAST_TILE), lambda i, j: (0, slab(i, j))),
            pl.BlockSpec((D_MODEL, FFN_CAST_TILE), lambda i, j: (0, slab(i, j))),
            pl.BlockSpec((FFN_CAST_TILE, D_MODEL), lambda i, j: (slab(i, j), 0)),
        ]
        out_shape = [
            jax.ShapeDtypeStruct((m, D_MODEL), F32),
            jax.ShapeDtypeStruct((D_MODEL, D_FF), BF16),
            jax.ShapeDtypeStruct((D_MODEL, D_FF), BF16),
            jax.ShapeDtypeStruct((D_FF, D_MODEL), BF16),
        ]
    return pl.pallas_call(
        functools.partial(_ffn_kernel, last=last, f32_weights=f32_weights, split_input=x_sample is not None),
        grid=grid,
        in_specs=in_specs,
        out_specs=out_specs,
        out_shape=out_shape,
        scratch_shapes=[pltpu.VMEM((FFN_ROW_TILE, D_MODEL), BF16)],
        compiler_params=_params("arbitrary", "arbitrary"),
        name="half_ffn",
    )(*operands)


def _gelu(z):
    return 0.5 * z * (1.0 + lax.erf(z * (1.0 / math.sqrt(2.0))))


def _log_sigmoid(z):
    return -(jnp.maximum(-z, 0.0) + jnp.log1p(jnp.exp(-jnp.abs(z))))


def _norm_matmul_kernel(x_ref, g_ref, w_ref, *rest, act, n_out, out_scale, split, row_tile, forget, cast_slabs):
    if forget:
        wf_ref, bf_ref, rest = rest[0], rest[1], rest[2:]
    n_cast = len(cast_slabs)
    cast_src, rest = rest[:n_cast], rest[n_cast:]
    outs, h_ref = rest[:n_out], rest[n_out]
    outs, cast_dst = outs[:n_out - n_cast], outs[n_out - n_cast:]
    j = pl.program_id(1)
    for src, dst, n_slabs in zip(cast_src, cast_dst, cast_slabs):
        @pl.when(pl.program_id(0) * pl.num_programs(1) + j < n_slabs)
        def _(src=src, dst=dst):
            dst[...] = src[...].astype(BF16)

    @pl.when(j == 0)
    def _():
        h_ref[...] = _rms(x_ref[...], g_ref[...]).astype(BF16)

    z = jnp.dot(h_ref[...], w_ref[...].astype(BF16), preferred_element_type=F32)
    if act == "gelu":
        z = _gelu(z)
    if forget:
        zf = jnp.dot(h_ref[...], wf_ref[...].astype(BF16), preferred_element_type=F32)
        outs[-1][...] = _log_sigmoid(zf + bf_ref[...])
        outs = outs[:-1]
    if out_scale != 1.0:
        z = z * out_scale
    if split:
        outs[0][...] = z

        @pl.when(pl.program_id(0) == pl.num_programs(0) - 1)
        def _():
            outs[1][...] = z[row_tile - N_SAMPLE:, :]
        outs = outs[2:]
    for o in outs:
        o[...] = z.astype(o.dtype)


def _norm_matmul(x, g, w, layer, row_tile, col_tile, act, out_dtypes, out_scale=1.0, split=False, forget=None,
                 casts=()):
    m = x.shape[0]
    n = w.shape[2]
    grid = (m // row_tile, n // col_tile)
    assert forget is None or col_tile == n
    w_mode = dict(pipeline_mode=pl.Buffered(1)) if col_tile == n else {}
    out_spec = pl.BlockSpec((row_tile, col_tile), lambda i, j: (i, j))
    out_specs = [out_spec] * len(out_dtypes)
    out_shape = [jax.ShapeDtypeStruct((m, n), d) for d in out_dtypes]
    if split:
        last = grid[0] - 1
        sample_spec = pl.BlockSpec((N_SAMPLE, col_tile), lambda i, j: (0, jnp.where(i == last, j, 0)))
        out_specs = [out_spec, sample_spec] + out_specs
        out_shape = [jax.ShapeDtypeStruct((SEQ, n), F32), jax.ShapeDtypeStruct((N_SAMPLE, n), F32)] + out_shape
    in_specs = [
        pl.BlockSpec((row_tile, D_MODEL), lambda i, j: (i, 0)),
        pl.BlockSpec((1, D_MODEL), lambda i, j: (0, 0)),
        pl.BlockSpec((None, D_MODEL, col_tile), lambda i, j: (layer, 0, j), **w_mode),
    ]
    operands = [x, g, w]
    if forget is not None:
        in_specs += [pl.BlockSpec((None, D_MODEL, LANES), lambda i, j: (0, 0, 0)),
                     pl.BlockSpec((1, LANES), lambda i, j: (0, 0))]
        operands += list(forget)
        out_specs = out_specs + [pl.BlockSpec((row_tile, LANES), lambda i, j: (i, 0))]
        out_shape = out_shape + [jax.ShapeDtypeStruct((m, LANES), F32)]
    cast_slabs = []
    for src, src_layer, axis in casts:
        rows, cols = src.shape[1:]
        n_slabs = (cols if axis == 1 else rows) // LANES
        slab = lambda i, j, n_slabs=n_slabs: jnp.minimum(i * grid[1] + j, n_slabs - 1)
        if axis == 1:
            in_specs.append(pl.BlockSpec((None, rows, LANES), lambda i, j, s=slab, l=src_layer: (l, 0, s(i, j))))
            out_specs = out_specs + [pl.BlockSpec((rows, LANES), lambda i, j, s=slab: (0, s(i, j)))]
        else:
            in_specs.append(pl.BlockSpec((None, LANES, cols), lambda i, j, s=slab, l=src_layer: (l, s(i, j), 0)))
            out_specs = out_specs + [pl.BlockSpec((LANES, cols), lambda i, j, s=slab: (s(i, j), 0))]
        operands.append(src)
        out_shape = out_shape + [jax.ShapeDtypeStruct((rows, cols), BF16)]
        cast_slabs.append(n_slabs)
    return pl.pallas_call(
        functools.partial(_norm_matmul_kernel, act=act, n_out=len(out_shape), out_scale=out_scale,
                          split=split, row_tile=row_tile, forget=forget is not None,
                          cast_slabs=tuple(cast_slabs)),
        grid=grid,
        in_specs=in_specs,
        out_specs=out_specs,
        out_shape=out_shape,
        scratch_shapes=[pltpu.VMEM((row_tile, D_MODEL), BF16)],
        compiler_params=_params("arbitrary" if split or casts else "parallel", "arbitrary"),
        name="norm_matmul_" + str(act),
    )(*operands)


def _matmul_residual_kernel(ap_ref, as_ref, w_ref, x_ref, o_ref):
    i, last_i = pl.program_id(0), pl.num_programs(0) - 1

    def project(a):
        o_ref[...] = x_ref[...] + jnp.dot(a, w_ref[...].astype(BF16), preferred_element_type=F32)

    @pl.when(i < last_i)
    def _():
        project(ap_ref[...])

    @pl.when(i == last_i)
    def _():
        project(jnp.concatenate([ap_ref[:SMALL_ROW_TILE - N_SAMPLE, :], as_ref[...]], axis=0))


def _matmul_residual(a_prompt, a_sample, w, layer, x):
    m = x.shape[0]
    k, n = w.shape[1:]
    return pl.pallas_call(
        _matmul_residual_kernel,
        grid=(m // SMALL_ROW_TILE,),
        in_specs=[
            pl.BlockSpec((SMALL_ROW_TILE, k), lambda i: (i, 0)),
            pl.BlockSpec((N_SAMPLE, k), lambda i: (0, 0)),
            pl.BlockSpec((None, k, n), lambda i: (layer, 0, 0), pipeline_mode=pl.Buffered(1)),
            pl.BlockSpec((SMALL_ROW_TILE, n), lambda i: (i, 0)),
        ],
        out_specs=pl.BlockSpec((SMALL_ROW_TILE, n), lambda i: (i, 0)),
        out_shape=jax.ShapeDtypeStruct((m, n), F32),
        compiler_params=_params("parallel"),
        name="matmul_residual",
    )(a_prompt, a_sample, w, x)


def _gmlp_mix_kernel(u_ref, v_ref, x_ref, lng_ref, lnb_ref, ws_ref, bs_ref, wo_ref,
                     o_ref, vn_ref, gated_ref, *, n_prompt_tiles):
    i = pl.program_id(0)
    v = v_ref[...]
    mu = jnp.mean(v, axis=-1, keepdims=True)
    vc = v - mu
    vn = vc * lax.rsqrt(jnp.mean(vc * vc, axis=-1, keepdims=True) + LN_EPS)
    vn = vn * lng_ref[...] + lnb_ref[...]

    is_sample = i >= n_prompt_tiles

    @pl.when(is_sample)
    def _():
        vn_ref[...] = vn

    sel = is_sample.astype(jnp.int32)
    shift = jnp.where(is_sample, int(math.log2(DEC_SEQ)), int(math.log2(GMLP_CHUNK)))
    t = lax.broadcasted_iota(jnp.int32, (GMLP_CHUNK, GMLP_CHUNK), 0)
    s = lax.broadcasted_iota(jnp.int32, (GMLP_CHUNK, GMLP_CHUNK), 1)
    mask = (s <= t) & ((t >> shift) == (s >> shift))
    vnb = vn.astype(BF16)
    for grp in range(GMLP_GROUPS):
        w = jnp.where(mask, ws_ref[sel, grp], 0.0).astype(BF16)
        bias = bs_ref[sel, grp]
        cols = slice(grp * GROUP_W, (grp + 1) * GROUP_W)
        for c in range(GMLP_ROW_TILE // GMLP_CHUNK):
            rows = slice(c * GMLP_CHUNK, (c + 1) * GMLP_CHUNK)
            mixed = jnp.dot(w, vnb[rows, cols], preferred_element_type=F32) + bias
            gated_ref[rows, cols] = (u_ref[rows, cols] * mixed).astype(BF16)
    o_ref[...] = x_ref[...] + jnp.dot(gated_ref[...], wo_ref[...], preferred_element_type=F32)


def _gmlp_mix(z, x, ln_g, ln_b, ws2, bs2, w_out):
    m = x.shape[0]
    n_prompt_tiles = SEQ // GMLP_ROW_TILE
    single = pl.Buffered(1)
    return pl.pallas_call(
        functools.partial(_gmlp_mix_kernel, n_prompt_tiles=n_prompt_tiles),
        grid=(m // GMLP_ROW_TILE,),
        in_specs=[
            pl.BlockSpec((GMLP_ROW_TILE, D_GATE), lambda i: (i, 0)),
            pl.BlockSpec((GMLP_ROW_TILE, D_GATE), lambda i: (i, 1)),
            pl.BlockSpec((GMLP_ROW_TILE, D_MODEL), lambda i: (i, 0)),
            pl.BlockSpec((1, D_GATE), lambda i: (0, 0)),
            pl.BlockSpec((1, D_GATE), lambda i: (0, 0)),
            pl.BlockSpec((2, GMLP_GROUPS, GMLP_CHUNK, GMLP_CHUNK), lambda i: (0, 0, 0, 0)),
            pl.BlockSpec((2, GMLP_GROUPS, GMLP_CHUNK, 1), lambda i: (0, 0, 0, 0)),
            pl.BlockSpec((D_GATE, D_MODEL), lambda i: (0, 0), pipeline_mode=single),
        ],
        out_specs=[
            pl.BlockSpec((GMLP_ROW_TILE, D_MODEL), lambda i: (i, 0)),
            pl.BlockSpec((GMLP_ROW_TILE, D_GATE),
                         lambda i: (jnp.maximum(i - n_prompt_tiles, 0), 0), pipeline_mode=single),
        ],
        out_shape=[
            jax.ShapeDtypeStruct((m, D_MODEL), F32),
            jax.ShapeDtypeStruct((N_SAMPLE, D_GATE), F32),
        ],
        scratch_shapes=[pltpu.VMEM((GMLP_ROW_TILE, D_GATE), BF16)],
        compiler_params=_params("arbitrary"),
        name="gmlp_mix",
    )(z, z, x, ln_g, ln_b, ws2, bs2, w_out)


def _scan_kernel(x_ref, o_ref):
    x = x_ref[...]
    n = x.shape[-1]
    idx = lax.broadcasted_iota(jnp.int32, x.shape, 1)
    step = 1
    while step < n:
        x = x + jnp.where(idx >= step, pltpu.roll(x, step, axis=1), 0.0)
        step *= 2
    o_ref[...] = x * LOG2E


def _cumsum_lanes(x):
    return pl.pallas_call(
        _scan_kernel,
        out_shape=jax.ShapeDtypeStruct(x.shape, F32),
        compiler_params=pltpu.CompilerParams(vmem_limit_bytes=VMEM_LIMIT),
        name="cumsum_lanes",
    )(x)


def _tile_lanes(x, reps):
    return jnp.concatenate([x] * reps, axis=1)


def _softmax_step(logits, v, m_ref, acc_ref):
    reps = logits.shape[1] // LANES
    m_prev = m_ref[...]
    m_new = jnp.maximum(m_prev, jnp.max(logits, axis=1, keepdims=True))
    p = jnp.exp2(logits - _tile_lanes(m_new, reps))
    alpha = jnp.exp2(m_prev - m_new)
    v_ones = jnp.concatenate([v, jnp.ones_like(v)], axis=1)
    pv = jnp.dot(p.astype(BF16), v_ones, preferred_element_type=F32)
    acc_ref[...] = _tile_lanes(alpha, 2) * acc_ref[...] + pv
    m_ref[...] = m_new


def _softmax_result(acc_ref):
    acc = acc_ref[...]
    return acc[:, :HEAD_DIM] / acc[:, HEAD_DIM:]


def _fox_prompt_kernel(q_ref, k_ref, v_ref, c_ref, cq_rows_ref, o_ref, m_ref, acc_ref, cq_ref):
    qi = pl.program_id(1)
    per_key_tile = ATT_TILE // ATT_Q_TILE
    diag = qi // per_key_tile
    first_row = (qi % per_key_tile) * ATT_Q_TILE
    for hh in range(HEADS_PER_STEP):
        cq_ref[hh] = jnp.broadcast_to(cq_rows_ref[hh, qi], (LANES, ATT_Q_TILE)).T
    m_ref[...] = jnp.full_like(m_ref, NEG_INF)
    acc_ref[...] = jnp.zeros_like(acc_ref)

    def block(kj, on_diagonal):
        start = pl.multiple_of(kj * ATT_TILE, ATT_TILE)
        for hh in range(HEADS_PER_STEP):
            cols = slice(hh * HEAD_DIM, (hh + 1) * HEAD_DIM)
            k = k_ref[pl.ds(start, ATT_TILE), cols]
            s = lax.dot_general(q_ref[:, cols], k, (((1,), (1,)), ((), ())), preferred_element_type=F32)
            logits = s + (_tile_lanes(cq_ref[hh], ATT_TILE // LANES) - c_ref[hh, kj])
            if on_diagonal:
                row = lax.broadcasted_iota(jnp.int32, (ATT_Q_TILE, ATT_TILE), 0) + first_row
                col = lax.broadcasted_iota(jnp.int32, (ATT_Q_TILE, ATT_TILE), 1)
                logits = jnp.where(row >= col, logits, NEG_INF)
            _softmax_step(logits, v_ref[pl.ds(start, ATT_TILE), cols], m_ref.at[hh], acc_ref.at[hh])

    def body(kj, carry):
        block(kj, False)
        return carry

    lax.fori_loop(0, diag, body, 0)
    block(diag, True)
    for hh in range(HEADS_PER_STEP):
        cols = slice(hh * HEAD_DIM, (hh + 1) * HEAD_DIM)
        o_ref[:, cols] = _softmax_result(acc_ref.at[hh]).astype(o_ref.dtype)


def _fox_prompt(q, k, v, c_tiles):
    n_tiles = SEQ // ATT_TILE
    n_q_tiles = SEQ // ATT_Q_TILE
    width = HEADS_PER_STEP * HEAD_DIM
    cq_rows = c_tiles.reshape(N_HEADS, n_q_tiles, 1, ATT_Q_TILE)
    return pl.pallas_call(
        _fox_prompt_kernel,
        grid=(N_HEADS // HEADS_PER_STEP, n_q_tiles),
        in_specs=[
            pl.BlockSpec((ATT_Q_TILE, width), lambda h, i: (i, h)),
            pl.BlockSpec((SEQ, width), lambda h, i: (0, h), pipeline_mode=pl.Buffered(1)),
            pl.BlockSpec((SEQ, width), lambda h, i: (0, h), pipeline_mode=pl.Buffered(1)),
            pl.BlockSpec((HEADS_PER_STEP, n_tiles, 1, ATT_TILE), lambda h, i: (h, 0, 0, 0)),
            pl.BlockSpec((HEADS_PER_STEP, n_q_tiles, 1, ATT_Q_TILE), lambda h, i: (h, 0, 0, 0)),
        ],
        out_specs=pl.BlockSpec((ATT_Q_TILE, width), lambda h, i: (i, h)),
        out_shape=jax.ShapeDtypeStruct((SEQ, D_MODEL), BF16),
        scratch_shapes=[
            pltpu.VMEM((HEADS_PER_STEP, ATT_Q_TILE, LANES), F32),
            pltpu.VMEM((HEADS_PER_STEP, ATT_Q_TILE, 2 * HEAD_DIM), F32),
            pltpu.VMEM((HEADS_PER_STEP, ATT_Q_TILE, LANES), F32),
        ],
        compiler_params=_params("parallel", "arbitrary"),
        name="fox_prompt",
    )(q, k, v, c_tiles, cq_rows)


def _fox_sample_kernel(q_ref, kc_ref, vc_ref, kn_ref, vn_ref, cc_ref, cn_ref, o_ref, *rest, repack):
    m_ref, acc_ref = rest[-2:]
    j = pl.program_id(1)

    @pl.when(j == 0)
    def _():
        m_ref[...] = jnp.full_like(m_ref, NEG_INF)
        acc_ref[...] = jnp.zeros_like(acc_ref)

    c_new = cn_ref[...]
    c_pad = jnp.concatenate([c_new, jnp.zeros((LANES - N_HEADS, LANES), F32)], axis=0)
    cq_all = c_pad.T[:DEC_SEQ, :]

    def cq_of(h):
        return jnp.broadcast_to(cq_all[:, h:h + 1], (DEC_SEQ, LANES))

    for h in range(N_HEADS):
        cols = slice(h * HEAD_DIM, (h + 1) * HEAD_DIM)
        if repack:
            head_rows = pl.ds(h, CACHE_TILE, stride=N_HEADS)
            k = kc_ref[head_rows, :].astype(BF16)
            v = vc_ref[head_rows, :].astype(BF16)
            rest[0][h] = k
            rest[1][h] = v
        else:
            k, v = kc_ref[h], vc_ref[h]
        s = lax.dot_general(q_ref[:, cols], k, (((1,), (1,)), ((), ())), preferred_element_type=F32)
        logits = s + (_tile_lanes(cq_of(h), cc_ref.shape[1] // LANES) - cc_ref[h:h + 1, :])
        _softmax_step(logits, v, m_ref.at[h], acc_ref.at[h])

    @pl.when(j == pl.num_programs(1) - 1)
    def _():
        row = lax.broadcasted_iota(jnp.int32, (DEC_SEQ, LANES), 0)
        col = lax.broadcasted_iota(jnp.int32, (DEC_SEQ, LANES), 1)
        pad = jnp.zeros((LANES - DEC_SEQ, HEAD_DIM), BF16)
        for h in range(N_HEADS):
            cols = slice(h * HEAD_DIM, (h + 1) * HEAD_DIM)
            k = jnp.concatenate([kn_ref[:, cols], pad], axis=0)
            v = jnp.concatenate([vn_ref[:, cols], pad], axis=0)
            s = lax.dot_general(q_ref[:, cols], k, (((1,), (1,)), ((), ())), preferred_element_type=F32)
            logits = s + (cq_of(h) - c_new[h:h + 1, :])
            logits = jnp.where(row >= col, logits, NEG_INF)
            _softmax_step(logits, v, m_ref.at[h], acc_ref.at[h])
            o_ref[:, cols] = _softmax_result(acc_ref.at[h]).astype(o_ref.dtype)


def _fox_sample(q, k_new, v_new, cache_k, cache_v, c_sample, repack):
    first = SEQ // DEC_SEQ
    tile = CACHE_TILE if repack else 2 * CACHE_TILE
    n_steps = PAST_LEN // tile
    new_spec = pl.BlockSpec((DEC_SEQ, D_MODEL), lambda b, j: (first + b, 0))
    head_spec = pl.BlockSpec((None, N_HEADS, tile, HEAD_DIM), lambda b, j: (b, 0, j, 0))
    out_specs = pl.BlockSpec((DEC_SEQ, D_MODEL), lambda b, j: (b, 0))
    out_shape = jax.ShapeDtypeStruct((N_SAMPLE, D_MODEL), BF16)
    if repack:
        cache_spec = pl.BlockSpec((None, CACHE_TILE * N_HEADS, HEAD_DIM), lambda b, j: (b, j, 0))
        cache_k = cache_k.reshape(DEC_BATCH, PAST_LEN * N_HEADS, HEAD_DIM)
        cache_v = cache_v.reshape(DEC_BATCH, PAST_LEN * N_HEADS, HEAD_DIM)
        per_head = jax.ShapeDtypeStruct((DEC_BATCH, N_HEADS, PAST_LEN, HEAD_DIM), BF16)
        out_specs, out_shape = [out_specs, head_spec, head_spec], [out_shape, per_head, per_head]
    else:
        cache_spec = head_spec
    return pl.pallas_call(
        functools.partial(_fox_sample_kernel, repack=repack),
        grid=(DEC_BATCH, n_steps),
        in_specs=[
            new_spec, cache_spec, cache_spec, new_spec, new_spec,
            pl.BlockSpec((None, N_HEADS, tile), lambda b, j: (b, 0, j)),
            pl.BlockSpec((None, N_HEADS, LANES), lambda b, j: (b, 0, PAST_LEN // LANES)),
        ],
        out_specs=out_specs,
        out_shape=out_shape,
        scratch_shapes=[
            pltpu.VMEM((N_HEADS, DEC_SEQ, LANES), F32),
            pltpu.VMEM((N_HEADS, DEC_SEQ, 2 * HEAD_DIM), F32),
        ],
        compiler_params=_params("parallel", "arbitrary"),
        name="fox_sample",
    )(q, cache_k, cache_v, k_new, v_new, c_sample, c_sample)


def kernel(x_prompt, x_sample, cache_k, cache_v, cache_logf, ffn1_norm, ffn1_w_gate, ffn1_w_up,
           ffn1_w_down, mix_norm, ffn2_norm, ffn2_w_gate, ffn2_w_up, ffn2_w_down, gmlp_w_in,
           gmlp_ln_g, gmlp_ln_b, gmlp_w_s, gmlp_b_s, gmlp_w_out, kv_norm, w_k, w_v, w_f, b_f,
           fox_w_q, fox_w_o, final_norm):
    bf = lambda w: w.astype(BF16)
    x = x_prompt.reshape(SEQ, D_MODEL)

    ffn1 = (ffn1_w_gate, ffn1_w_up, ffn1_w_down)
    ffn2 = (ffn2_w_gate, ffn2_w_up, ffn2_w_down)
    final_g = final_norm.reshape(1, D_MODEL)
    w_in_bf, w_out_bf = bf(gmlp_w_in[0])[None], bf(gmlp_w_out[0])

    reps = GMLP_CHUNK // DEC_SEQ
    ws2 = jnp.stack([gmlp_w_s, jnp.tile(gmlp_w_s[:, :, :DEC_SEQ, :DEC_SEQ], (1, 1, reps, reps))], axis=1)
    bs2 = jnp.stack([gmlp_b_s, jnp.tile(gmlp_b_s[:, :, :DEC_SEQ], (1, 1, reps))], axis=1)[..., None]

    gmlp_v = []
    logf_all = None
    for l in range(DEPTH):
        if l == N_A_LAYERS:
            kvg = kv_norm.reshape(1, D_MODEL)
            w_f_pad = jnp.pad(w_f, ((0, 0), (0, LANES - N_HEADS)))[None]
            b_f_pad = jnp.pad(b_f, (0, LANES - N_HEADS)).reshape(1, LANES)
            k_prompt, k_sample, k_bf = _norm_matmul(x, kvg, w_k[None], 0, SMALL_ROW_TILE, D_MODEL, None,
                                                    (BF16,), split=True)
            v_prompt, v_sample, v_bf, logf_pad = _norm_matmul(x, kvg, w_v[None], 0, SMALL_ROW_TILE, D_MODEL, None,
                                                              (BF16,), split=True, forget=(w_f_pad, b_f_pad))
            logf_all = logf_pad[:, :N_HEADS]
            c_prompt = _cumsum_lanes(logf_all[:SEQ].T)
            c_prompt = c_prompt.reshape(N_HEADS, SEQ // ATT_TILE, 1, ATT_TILE)
            lf_new = logf_all[SEQ:].reshape(DEC_BATCH, DEC_SEQ, N_HEADS).transpose(0, 2, 1)
            lf_cache = cache_logf.transpose(0, 2, 1)
            lf = jnp.concatenate(
                [lf_cache, lf_new, jnp.zeros((DEC_BATCH, N_HEADS, LANES - DEC_SEQ), F32)], axis=-1)
            c_sample = _cumsum_lanes(lf.reshape(DEC_BATCH * N_HEADS, PAST_LEN + LANES))
            c_sample = c_sample.reshape(DEC_BATCH, N_HEADS, PAST_LEN + LANES)

        first = (dict(weights=ffn1, layer=0, x_sample=x_sample.reshape(N_SAMPLE, D_MODEL)) if l == 0
                 else dict(weights=ffn_bf))
        x, *ffn_bf = _half_ffn(x, ffn1_norm[l].reshape(1, D_MODEL), next_weights=ffn2, next_layer=l, **first)
        mg = mix_norm[l].reshape(1, D_MODEL)
        if l < N_A_LAYERS:
            later = ((gmlp_w_in, l + 1, 1), (gmlp_w_out, l + 1, 0)) if l + 1 < N_A_LAYERS else ()
            z, *rounded = _norm_matmul(x, mg, w_in_bf, 0, ROW_TILE, 1024, "gelu", (F32,), casts=later)
            x, vn = _gmlp_mix(z, x, gmlp_ln_g[l].reshape(1, D_GATE), gmlp_ln_b[l].reshape(1, D_GATE),
                              ws2[l], bs2[l], w_out_bf)
            if rounded:
                w_in_bf, w_out_bf = rounded[0][None], rounded[1]
            gmlp_v.append(vn.reshape(DEC_BATCH, DEC_SEQ, D_GATE))
        else:
            jj = l - N_A_LAYERS
            (q,) = _norm_matmul(x, mg, fox_w_q, jj, SMALL_ROW_TILE, D_MODEL, None, (BF16,),
                                out_scale=HEAD_DIM ** -0.5 * LOG2E)
            o_prompt = _fox_prompt(q, k_bf, v_bf, c_prompt)
            if jj == 0:
                o_sample, cache_k_heads, cache_v_heads = _fox_sample(q, k_bf, v_bf, cache_k, cache_v, c_sample, True)
            else:
                o_sample = _fox_sample(q, k_bf, v_bf, cache_k_heads, cache_v_heads, c_sample, False)
            x = _matmul_residual(o_prompt, o_sample, fox_w_o, jj, x)
        g2 = ffn2_norm[l].reshape(1, D_MODEL)
        if l < DEPTH - 1:
            x, *ffn_bf = _half_ffn(x, g2, ffn_bf, ffn1, l + 1)
        else:
            x = _half_ffn(x, g2, ffn_bf, final_gain=final_g)

    y_prompt, y_sample = x
    hd = (N_HEADS, HEAD_DIM)
    return (y_prompt.reshape(1, SEQ, D_MODEL),
            y_sample.reshape(DEC_BATCH, DEC_SEQ, D_MODEL),
            k_prompt.reshape(1, SEQ, *hd),
            v_prompt.reshape(1, SEQ, *hd),
            logf_all[:SEQ].reshape(1, SEQ, N_HEADS),
            k_sample.reshape(DEC_BATCH, DEC_SEQ, *hd),
            v_sample.reshape(DEC_BATCH, DEC_SEQ, *hd),
            logf_all[SEQ:].reshape(DEC_BATCH, DEC_SEQ, N_HEADS),
            jnp.stack(gmlp_v, axis=0))
```
